```python
import jax, jax.numpy as jnp
from jax import lax
import numpy as np

D_MODEL = 4096
BATCH = 4
SEQ = 2048
DEPTH = 2
DEC_BATCH = 8
DEC_SEQ = 8
PAST_LEN = 16384
PAGE_SIZE = 128

HEAD_DIM = 128
W_A = D_MODEL // 2
W_B = D_MODEL // 4
W_C = D_MODEL - W_A - W_B
N_HEADS_A = W_A // HEAD_DIM
N_KV_A = N_HEADS_A // 4
KV_W = N_KV_A * HEAD_DIM
N_GROUPS_B = W_B // HEAD_DIM
N_GROUPS_C = W_C // HEAD_DIM
MOBA_BLOCK = 256
MOBA_TOPK = 3
Q_CHUNK = 8
SGU_CHUNK = 128
CONV_W = 31
D_FF = 4 * D_MODEL
P_IN = W_A + 2 * KV_W + 2 * W_B + 2 * W_C
SPLITS = (W_A, W_A + KV_W, W_A + 2 * KV_W, W_A + 2 * KV_W + W_B, W_A + 2 * KV_W + 2 * W_B)
RMS_EPS = 1e-6
LN_EPS = 1e-5

kernel_name = "hymba_moba_gmlp_conformer_decode_step"


def rms_norm(x, g):
    xf = x.astype(jnp.float32)
    y = xf * lax.rsqrt(jnp.mean(xf * xf, axis=-1, keepdims=True) + RMS_EPS)
    return (y * g.astype(jnp.float32)).astype(x.dtype)


def group_layer_norm(x, g, b, n_groups):
    shp = x.shape
    xf = x.astype(jnp.float32).reshape(shp[:-1] + (n_groups, shp[-1] // n_groups))
    mu = jnp.mean(xf, axis=-1, keepdims=True)
    var = jnp.mean(jnp.square(xf - mu), axis=-1, keepdims=True)
    y = ((xf - mu) * lax.rsqrt(var + LN_EPS)).reshape(shp)
    return (y * g.astype(jnp.float32) + b.astype(jnp.float32)).astype(x.dtype)


def moba_blocks(k, v):
    B, T, Hkv, D = k.shape
    nb = -(-T // MOBA_BLOCK)
    pad = nb * MOBA_BLOCK - T
    padw = ((0, 0), (0, pad), (0, 0), (0, 0))
    kb = jnp.pad(k, padw).reshape(B, nb, MOBA_BLOCK, Hkv, D).transpose(0, 3, 1, 2, 4)
    vb = jnp.pad(v, padw).reshape(B, nb, MOBA_BLOCK, Hkv, D).transpose(0, 3, 1, 2, 4)
    k_mean = jnp.mean(kb.astype(jnp.float32), axis=3)
    return kb, vb, k_mean


def moba_attend(q, q_pos, kb, vb, k_mean):
    B, Q, H, D = q.shape
    Hkv, nb = kb.shape[1], kb.shape[2]
    G = H // Hkv
    qg = q.reshape(B, Q, Hkv, G, D).transpose(0, 2, 3, 1, 4)
    own = q_pos // MOBA_BLOCK
    gate = jnp.einsum('bkgqd,bknd->bkgqn', qg.astype(jnp.float32), k_mean)
    fully_past = jnp.arange(nb)[None, :] < own[:, None]
    gate = jnp.where(fully_past, gate, -jnp.inf)
    k_eff = min(MOBA_TOPK, nb)
    _, top_idx = lax.top_k(gate, k_eff)
    top_valid = top_idx < own[:, None]
    own_idx = jnp.broadcast_to(own[:, None], top_idx.shape[:-1] + (1,)).astype(top_idx.dtype)
    sel = jnp.concatenate([top_idx, own_idx], axis=-1)
    sel_valid = jnp.concatenate([top_valid, jnp.ones(top_valid.shape[:-1] + (1,), bool)], axis=-1)
    gather = jax.vmap(jax.vmap(lambda blocks, ix: blocks[ix]))
    k_sel = gather(kb, sel)
    v_sel = gather(vb, sel)
    key_pos = sel[..., None] * MOBA_BLOCK + jnp.arange(MOBA_BLOCK, dtype=sel.dtype)
    valid = sel_valid[..., None] & (key_pos <= q_pos[:, None, None])
    s = jnp.einsum('bkgqd,bkgqnjd->bkgqnj', qg, k_sel).astype(jnp.float32) * (HEAD_DIM ** -0.5)
    s = jnp.where(valid, s, -jnp.inf)
    shp = s.shape
    p = jax.nn.softmax(s.reshape(shp[:-2] + (shp[-2] * shp[-1],)), axis=-1).reshape(shp)
    o = jnp.einsum('bkgqnj,bkgqnjd->bkgqd', p, v_sel.astype(jnp.float32))
    return o.transpose(0, 3, 1, 2, 4).reshape(B, Q, H * D).astype(q.dtype)


def moba_sweep(q, q_pos, kb, vb, k_mean, qc):
    B, L, H, D = q.shape
    n = L // qc
    qs = q.reshape(B, n, qc, H, D).transpose(1, 0, 2, 3, 4)
    ps = q_pos.reshape(n, qc)
    out = lax.map(lambda a: moba_attend(a[0], a[1], kb, vb, k_mean), (qs, ps))
    return out.transpose(1, 0, 2, 3).reshape(B, L, H * D)


def chunk_spatial_mix(vn, w_s, b_s):
    B, L, NG, GD = vn.shape
    nc = -(-L // SGU_CHUNK)
    vp = jnp.pad(vn, ((0, 0), (0, nc * SGU_CHUNK - L), (0, 0), (0, 0))).reshape(B, nc, SGU_CHUNK, NG, GD)
    mask = jnp.tril(jnp.ones((SGU_CHUNK, SGU_CHUNK), bool))
    w = jnp.where(mask[None], w_s, 0.0).astype(vn.dtype)
    s = jnp.einsum('gts,bcsgd->bctgd', w, vp) + b_s.T.astype(vn.dtype)[None, None, :, :, None]
    return s.reshape(B, nc * SGU_CHUNK, NG, GD)[:, :L]


def sgu_branch(bu, bv, p):
    B, L, _ = bv.shape
    u = jax.nn.gelu(bu)
    vn = group_layer_norm(jax.nn.gelu(bv), p['sgu_norm_g'], p['sgu_norm_b'], N_GROUPS_B)
    s = chunk_spatial_mix(vn.reshape(B, L, N_GROUPS_B, W_B // N_GROUPS_B), p['sgu_w'], p['sgu_b'])
    return u * s.reshape(B, L, W_B), vn


def glu(cg):
    a, gate = jnp.split(cg, 2, axis=-1)
    return a * jax.nn.sigmoid(gate)


def conv_branch(conv_in, p):
    w = p['conv_w'].astype(conv_in.dtype)[:, None, :]
    y = lax.conv_general_dilated(conv_in, w, (1,), 'VALID',
                                 dimension_numbers=('NWC', 'WIO', 'NWC'),
                                 feature_group_count=W_C)
    y = y + p['conv_b'].astype(y.dtype)
    y = group_layer_norm(y, p['conv_norm_g'], p['conv_norm_b'], N_GROUPS_C)
    return jax.nn.silu(y)


def in_proj(x, p):
    h = rms_norm(x, p['g_mix_pre'])
    return jnp.split(h @ p['w_in'], SPLITS, axis=-1)


def finish_layer(x, attn, sgu, conv, p):
    mix = jnp.concatenate([attn, sgu, conv], axis=-1) @ p['w_out']
    x = x + rms_norm(mix, p['g_mix_post'])
    h = rms_norm(x, p['g_ffn_pre'])
    m = jnp.square(jax.nn.relu(h @ p['w_up'])) @ p['w_down']
    return x + rms_norm(m, p['g_ffn_post'])


def prompt_layer(x, p):
    B, S, _ = x.shape
    q, k, v, bu, bv, cg = in_proj(x, p)
    q = q.reshape(B, S, N_HEADS_A, HEAD_DIM)
    k = k.reshape(B, S, N_KV_A, HEAD_DIM)
    v = v.reshape(B, S, N_KV_A, HEAD_DIM)
    kb, vb, km = moba_blocks(k, v)
    attn = moba_sweep(q, jnp.arange(S, dtype=jnp.int32), kb, vb, km, Q_CHUNK)
    sgu, _ = sgu_branch(bu, bv, p)
    conv_in = jnp.pad(glu(cg), ((0, 0), (CONV_W - 1, 0), (0, 0)))
    conv = conv_branch(conv_in, p)
    y = finish_layer(x, attn, sgu, conv, p)
    return y, k, v, conv_in[:, -(CONV_W - 1):]


def sample_layer(x, k_pages, v_pages, page_table, conv_state, p):
    B, L, _ = x.shape
    past = page_table.shape[1] * k_pages.shape[1]
    q, k, v, bu, bv, cg = in_proj(x, p)
    q = q.reshape(B, L, N_HEADS_A, HEAD_DIM)
    k = k.reshape(B, L, N_KV_A, HEAD_DIM)
    v = v.reshape(B, L, N_KV_A, HEAD_DIM)
    k_past = k_pages[page_table].reshape(B, past, N_KV_A, HEAD_DIM).astype(k.dtype)
    v_past = v_pages[page_table].reshape(B, past, N_KV_A, HEAD_DIM).astype(v.dtype)
    kb, vb, km = moba_blocks(jnp.concatenate([k_past, k], axis=1), jnp.concatenate([v_past, v], axis=1))
    attn = moba_sweep(q, past + jnp.arange(L, dtype=jnp.int32), kb, vb, km, 1)
    sgu, vn = sgu_branch(bu, bv, p)
    g = glu(cg)
    conv_in = jnp.concatenate([conv_state.astype(g.dtype), g], axis=1)
    conv = conv_branch(conv_in, p)
    y = finish_layer(x, attn, sgu, conv, p)
    return y, k, v, conv_in[:, -(CONV_W - 1):], vn


def setup_inputs(seed: int = 0) -> dict:
    key = jax.random.key(seed)
    ks = jax.random.split(key, 24)
    f32 = jnp.float32
    n_pages = PAST_LEN // PAGE_SIZE
    n_used = DEC_BATCH * n_pages
    n_phys = n_used + max(1, n_used // 4)
    nrm = lambda k, shape, scale: scale * jax.random.normal(k, shape, f32)
    gain = lambda k, shape: 1.0 + 0.05 * jax.random.normal(k, shape, f32)
    page_table = jax.random.permutation(ks[5], n_phys)[:n_used].reshape(DEC_BATCH, n_pages).astype(jnp.int32)
    return {
        'x_prompt': nrm(ks[0], (BATCH, SEQ, D_MODEL), 1.0),
        'x_sample': nrm(ks[1], (DEC_BATCH, DEC_SEQ, D_MODEL), 1.0),
        'cache_k': nrm(ks[2], (DEPTH, n_phys, PAGE_SIZE, N_KV_A, HEAD_DIM), 1.0),
        'cache_v': nrm(ks[3], (DEPTH, n_phys, PAGE_SIZE, N_KV_A, HEAD_DIM), 1.0),
        'state_conv': nrm(ks[4], (DEPTH, DEC_BATCH, CONV_W - 1, W_C), 0.5),
        'page_table': page_table,
        'g_mix_pre': gain(ks[6], (DEPTH, D_MODEL)),
        'w_in': nrm(ks[7], (DEPTH, D_MODEL, P_IN), D_MODEL ** -0.5),
        'sgu_norm_g': gain(ks[8], (DEPTH, W_B)),
        'sgu_norm_b': nrm(ks[9], (DEPTH, W_B), 0.02),
        'sgu_w': nrm(ks[10], (DEPTH, N_GROUPS_B, SGU_CHUNK, SGU_CHUNK), 0.5 * SGU_CHUNK ** -0.5),
        'sgu_b': gain(ks[11], (DEPTH, N_GROUPS_B, SGU_CHUNK)),
        'conv_w': nrm(ks[12], (DEPTH, CONV_W, W_C), CONV_W ** -0.5),
        'conv_b': nrm(ks[13], (DEPTH, W_C), 0.02),
        'conv_norm_g': gain(ks[14], (DEPTH, W_C)),
        'conv_norm_b': nrm(ks[15], (DEPTH, W_C), 0.02),
        'w_out': nrm(ks[16], (DEPTH, D_MODEL, D_MODEL), D_MODEL ** -0.5),
        'g_mix_post': gain(ks[17], (DEPTH, D_MODEL)),
        'g_ffn_pre': gain(ks[18], (DEPTH, D_MODEL)),
        'w_up': nrm(ks[19], (DEPTH, D_MODEL, D_FF), D_MODEL ** -0.5),
        'w_down': nrm(ks[20], (DEPTH, D_FF, D_MODEL), D_FF ** -0.5),
        'g_ffn_post': gain(ks[21], (DEPTH, D_MODEL)),
    }


def reference(x_prompt, x_sample, cache_k, cache_v, state_conv, page_table,
              g_mix_pre, w_in, sgu_norm_g, sgu_norm_b, sgu_w, sgu_b,
              conv_w, conv_b, conv_norm_g, conv_norm_b, w_out, g_mix_post,
              g_ffn_pre, w_up, w_down, g_ffn_post):
    yp, ys = x_prompt, x_sample
    kp, vp, cp, ksl, vsl, csl, usl = [], [], [], [], [], [], []
    for l in range(DEPTH):
        p = {'g_mix_pre': g_mix_pre[l], 'w_in': w_in[l],
             'sgu_norm_g': sgu_norm_g[l], 'sgu_norm_b': sgu_norm_b[l],
             'sgu_w': sgu_w[l], 'sgu_b': sgu_b[l],
             'conv_w': conv_w[l], 'conv_b': conv_b[l],
             'conv_norm_g': conv_norm_g[l], 'conv_norm_b': conv_norm_b[l],
             'w_out': w_out[l], 'g_mix_post': g_mix_post[l],
             'g_ffn_pre': g_ffn_pre[l], 'w_up': w_up[l], 'w_down': w_down[l],
             'g_ffn_post': g_ffn_post[l]}
        yp, k_n, v_n, c_n = prompt_layer(yp, p)
        kp.append(k_n); vp.append(v_n); cp.append(c_n)
        ys, k_s, v_s, c_s, u_s = sample_layer(ys, cache_k[l], cache_v[l], page_table, state_conv[l], p)
        ksl.append(k_s); vsl.append(v_s); csl.append(c_s); usl.append(u_s)
    return (yp, ys, jnp.stack(kp), jnp.stack(vp), jnp.stack(cp),
            jnp.stack(ksl), jnp.stack(vsl), jnp.stack(csl), jnp.stack(usl))
```

```python
import functools

import jax
import jax.numpy as jnp
from jax import lax
from jax.experimental import pallas as pl
from jax.experimental.pallas import tpu as pltpu

F32 = jnp.float32
BF16 = jnp.bfloat16

HEAD_DIM = 128
KV_GROUP = 4
MOBA_BLOCK = 256
MOBA_TOPK = 3
SGU_CHUNK = 128
CONV_W = 31
RMS_EPS = 1e-6
LN_EPS = 1e-5
MASKED = -1e30
LANES = 128
SUBLANES = 8
MIB = 1024 * 1024


def _cparams(sem, vmem_mib):
    return pltpu.CompilerParams(dimension_semantics=sem, vmem_limit_bytes=vmem_mib * MIB)


def _rms(x, g):
    ms = jnp.mean(x * x, axis=-1, keepdims=True)
    return x * lax.rsqrt(ms + RMS_EPS) * g


def _group_ln(x, g, b):
    mu = jnp.mean(x, axis=-1, keepdims=True)
    d = x - mu
    var = jnp.mean(d * d, axis=-1, keepdims=True)
    return d * lax.rsqrt(var + LN_EPS) * g + b


def _dot(a, b):
    return jnp.dot(a, b, preferred_element_type=F32)


def _dot_nt(a, b):
    return lax.dot_general(a, b, (((1,), (1,)), ((), ())), preferred_element_type=F32)


def _rms_inproj_kernel(x_ref, g_ref, w_ref, o_ref, h_ref):
    @pl.when(pl.program_id(1) == 0)
    def _():
        h_ref[...] = _rms(x_ref[...], g_ref[...]).astype(BF16)

    o_ref[...] = _dot(h_ref[...], w_ref[...])


def rms_inproj(x, g, w, layer, tm, tn):
    m, d = x.shape
    n = w.shape[-1]
    return pl.pallas_call(
        _rms_inproj_kernel,
        grid=(m // tm, n // tn),
        in_specs=[
            pl.BlockSpec((tm, d), lambda i, j: (i, 0)),
            pl.BlockSpec((None, 1, d), lambda i, j: (layer, 0, 0)),
            pl.BlockSpec((None, d, tn), lambda i, j: (layer, 0, j)),
        ],
        out_specs=pl.BlockSpec((tm, tn), lambda i, j: (i, j)),
        out_shape=jax.ShapeDtypeStruct((m, n), F32),
        scratch_shapes=[pltpu.VMEM((tm, d), BF16)],
        compiler_params=_cparams(("parallel", "arbitrary"), 48),
        name=f"rms_inproj_m{m}",
    )(x, g, w)


def _moba_prompt_kernel(q_ref, k_ref, v_ref, o_ref, kb_ref, vt_ref, km_ref, sel_ref, *, nb):
    qi = pl.program_id(2)
    blk = MOBA_BLOCK
    rows = KV_GROUP * blk

    @pl.when(qi == 0)
    def _():
        k = k_ref[...]
        kb_ref[...] = k.astype(BF16)
        vt_ref[...] = v_ref[...].T.astype(BF16)
        km_ref[...] = jnp.zeros_like(km_ref)
        for n in range(nb):
            km_ref[n:n + 1, :] = jnp.mean(k[n * blk:(n + 1) * blk], axis=0, keepdims=True)

    qs = q_ref[...]
    qall = jnp.concatenate([qs[:, g * HEAD_DIM:(g + 1) * HEAD_DIM] for g in range(KV_GROUP)], axis=0).astype(BF16)

    gate = _dot_nt(km_ref[...].astype(BF16), qall)
    n_iota = lax.broadcasted_iota(jnp.int32, gate.shape, 0)
    valid = n_iota < qi
    gv = jnp.where(valid, gate, -jnp.inf)
    rank = jnp.zeros(gate.shape, jnp.int32)
    for j in range(nb):
        gj = gv[j:j + 1, :]
        beats = jnp.where(gj > gv, 1, jnp.where((gj == gv) & (n_iota > j), 1, 0))
        rank = rank + beats
    sel_ref[...] = jnp.where(valid & (rank < MOBA_TOPK), 1.0, 0.0)

    scale = HEAD_DIM ** -0.5
    own = pl.multiple_of(qi * blk, blk)
    st = _dot_nt(kb_ref[pl.ds(own, blk), :], qall) * scale
    key_i = lax.broadcasted_iota(jnp.int32, st.shape, 0)
    row_i = lax.broadcasted_iota(jnp.int32, st.shape, 1) & (blk - 1)
    st = jnp.where(key_i <= row_i, st, MASKED)
    m0 = jnp.max(st, axis=0, keepdims=True)
    p0 = jnp.exp(st - m0)
    l0 = jnp.sum(p0, axis=0, keepdims=True)
    acc0 = _dot(vt_ref[:, pl.ds(own, blk)], p0.astype(BF16))

    def body(j, carry):
        m, l, acc = carry
        start = pl.multiple_of(j * blk, blk)
        s = _dot_nt(kb_ref[pl.ds(start, blk), :], qall) * scale
        s = jnp.where(sel_ref[pl.ds(j, 1), :] > 0.5, s, MASKED)
        m_new = jnp.maximum(m, jnp.max(s, axis=0, keepdims=True))
        alpha = jnp.exp(m - m_new)
        p = jnp.exp(s - m_new)
        l = alpha * l + jnp.sum(p, axis=0, keepdims=True)
        acc = alpha * acc + _dot(vt_ref[:, pl.ds(start, blk)], p.astype(BF16))
        return m_new, l, acc

    _, l, acc = lax.fori_loop(0, qi, body, (m0, l0, acc0))
    out = (acc / l).T
    for g in range(KV_GROUP):
        o_ref[:, g * HEAD_DIM:(g + 1) * HEAD_DIM] = out[g * blk:(g + 1) * blk, :].astype(o_ref.dtype)


def moba_prompt(proj, batch, seq, w_a, kv_w):
    assert seq % MOBA_BLOCK == 0
    nq = seq // MOBA_BLOCK
    n_kv = kv_w // HEAD_DIM
    gw = KV_GROUP * HEAD_DIM
    k_col = w_a // HEAD_DIM
    v_col = (w_a + kv_w) // HEAD_DIM
    nb_pad = -(-nq // SUBLANES) * SUBLANES
    return pl.pallas_call(
        functools.partial(_moba_prompt_kernel, nb=nq),
        grid=(batch, n_kv, nq),
        in_specs=[
            pl.BlockSpec((MOBA_BLOCK, gw), lambda b, h, i: (b * nq + i, h)),
            pl.BlockSpec((seq, HEAD_DIM), lambda b, h, i: (b, k_col + h)),
            pl.BlockSpec((seq, HEAD_DIM), lambda b, h, i: (b, v_col + h)),
        ],
        out_specs=pl.BlockSpec((MOBA_BLOCK, gw), lambda b, h, i: (b * nq + i, h)),
        out_shape=jax.ShapeDtypeStruct((batch * seq, w_a), BF16),
        scratch_shapes=[
            pltpu.VMEM((seq, HEAD_DIM), BF16),
            pltpu.VMEM((HEAD_DIM, seq), BF16),
            pltpu.VMEM((nb_pad, HEAD_DIM), F32),
            pltpu.VMEM((nb_pad, KV_GROUP * MOBA_BLOCK), F32),
        ],
        compiler_params=_cparams(("parallel", "parallel", "arbitrary"), 32),
        name="moba_prompt",
    )(proj, proj, proj)


def _moba_sample_kernel(pt_ref, q_ref, kn_ref, vn_ref, *rest, gp, n_grp, n_past_blk, dec_seq, n_heads):
    del pt_ref
    k_refs = rest[:gp]
    v_refs = rest[gp:2 * gp]
    o_ref = rest[2 * gp]
    qrow_ref, s_ref, km_ref, vpad_ref, m_ref, l_ref, acc_ref = rest[2 * gp + 1:]
    ph = pl.program_id(1)
    g = pl.program_id(2)
    page = k_refs[0].shape[0]
    kv_w = k_refs[0].shape[1]
    hq = n_heads * dec_seq
    chunk = gp * page
    blk_per_chunk = chunk // MOBA_BLOCK
    n_past = n_past_blk * MOBA_BLOCK
    scale = HEAD_DIM ** -0.5

    @pl.when((ph == 0) & (g == 0))
    def _():
        qrow_ref[...] = jnp.zeros_like(qrow_ref)
        km_ref[...] = jnp.zeros_like(km_ref)
        q = q_ref[...]
        for h in range(n_heads):
            kv = h // KV_GROUP
            qrow_ref[h * dec_seq:(h + 1) * dec_seq, kv * HEAD_DIM:(kv + 1) * HEAD_DIM] = (
                q[:, h * HEAD_DIM:(h + 1) * HEAD_DIM].astype(BF16))

    @pl.when(ph == 0)
    def _():
        kc = jnp.concatenate([r[...] for r in k_refs], axis=0)
        for n in range(blk_per_chunk):
            km_ref[pl.ds(g * blk_per_chunk + n, 1), :] = jnp.mean(
                kc[n * MOBA_BLOCK:(n + 1) * MOBA_BLOCK], axis=0, keepdims=True)
        s = _dot_nt(qrow_ref[...], kc.astype(BF16)) * scale
        s_ref[:, pl.ds(pl.multiple_of(g * chunk, chunk), chunk)] = s

    @pl.when((ph == 1) & (g == 0))
    def _():
        qrow = qrow_ref[...]
        gate = _dot_nt(qrow, km_ref[...].astype(BF16))
        n_iota = lax.broadcasted_iota(jnp.int32, gate.shape, 1)
        rank = jnp.zeros(gate.shape, jnp.int32)
        for j in range(n_past_blk):
            gj = gate[:, j:j + 1]
            rank = rank + jnp.where(gj > gate, 1, jnp.where((gj == gate) & (n_iota > j), 1, 0))
        sel = jnp.where(rank < MOBA_TOPK, 1.0, 0.0)
        m = jnp.full((hq, 1), MASKED, F32)
        for n in range(n_past_blk):
            sb = s_ref[:, n * MOBA_BLOCK:(n + 1) * MOBA_BLOCK]
            sb = jnp.where(sel[:, n:n + 1] > 0.5, sb, MASKED)
            s_ref[:, n * MOBA_BLOCK:(n + 1) * MOBA_BLOCK] = sb
            m = jnp.maximum(m, jnp.max(sb, axis=-1, keepdims=True))
        vpad_ref[...] = jnp.zeros_like(vpad_ref)
        vpad_ref[0:dec_seq, :] = vn_ref[...].astype(BF16)
        kpad = jnp.concatenate(
            [kn_ref[...], jnp.zeros((LANES - dec_seq, kv_w), F32)], axis=0).astype(BF16)
        so = _dot_nt(qrow, kpad) * scale
        key_t = lax.broadcasted_iota(jnp.int32, so.shape, 1)
        q_t = lax.broadcasted_iota(jnp.int32, so.shape, 0) % dec_seq
        so = jnp.where(key_t <= q_t, so, MASKED)
        m = jnp.maximum(m, jnp.max(so, axis=-1, keepdims=True))
        po = jnp.exp(so - m)
        m_ref[...] = m
        l_ref[...] = jnp.sum(po, axis=-1, keepdims=True)
        acc_ref[...] = _dot(po.astype(BF16), vpad_ref[...])

    @pl.when(ph == 1)
    def _():
        vc = jnp.concatenate([r[...] for r in v_refs], axis=0).astype(BF16)
        s = s_ref[:, pl.ds(pl.multiple_of(g * chunk, chunk), chunk)]
        p = jnp.exp(s - m_ref[...])
        l_ref[...] += jnp.sum(p, axis=-1, keepdims=True)
        acc_ref[...] += _dot(p.astype(BF16), vc)

    @pl.when((ph == 1) & (g == n_grp - 1))
    def _():
        out = acc_ref[...] / l_ref[...]
        for h in range(n_heads):
            kv = h // KV_GROUP
            o_ref[:, h * HEAD_DIM:(h + 1) * HEAD_DIM] = out[h * dec_seq:(h + 1) * dec_seq,
                                                            kv * HEAD_DIM:(kv + 1) * HEAD_DIM]
    del n_past


def moba_sample(proj, cache_k, cache_v, page_table, layer, dec_batch, dec_seq, w_a, kv_w):
    page = cache_k.shape[2]
    n_pages = page_table.shape[1]
    past = n_pages * page
    assert past % MOBA_BLOCK == 0 and dec_seq <= MOBA_BLOCK and dec_seq % SUBLANES == 0
    n_past_blk = past // MOBA_BLOCK
    n_heads = w_a // HEAD_DIM
    hq = n_heads * dec_seq
    gp = 8
    while n_pages % gp or (gp * page) % MOBA_BLOCK:
        gp //= 2
    n_grp = n_pages // gp
    km_rows = -(-n_past_blk // SUBLANES) * SUBLANES

    def k_map(i):
        return lambda b, ph, g, pt: (layer, pt[b, jnp.where(ph == 0, g, n_grp - 1) * gp + i], 0, 0)

    def v_map(i):
        return lambda b, ph, g, pt: (layer, pt[b, jnp.where(ph == 0, 0, g) * gp + i], 0, 0)

    k_col = w_a // kv_w
    grid_spec = pltpu.PrefetchScalarGridSpec(
        num_scalar_prefetch=1,
        grid=(dec_batch, 2, n_grp),
        in_specs=[
            pl.BlockSpec((dec_seq, w_a), lambda b, ph, g, pt: (b, 0)),
            pl.BlockSpec((dec_seq, kv_w), lambda b, ph, g, pt: (b, k_col)),
            pl.BlockSpec((dec_seq, kv_w), lambda b, ph, g, pt: (b, k_col + 1)),
        ] + [pl.BlockSpec((None, None, page, kv_w), k_map(i)) for i in range(gp)]
          + [pl.BlockSpec((None, None, page, kv_w), v_map(i)) for i in range(gp)],
        out_specs=pl.BlockSpec((dec_seq, w_a), lambda b, ph, g, pt: (b, 0)),
        scratch_shapes=[
            pltpu.VMEM((hq, kv_w), BF16),
            pltpu.VMEM((hq, past), F32),
            pltpu.VMEM((km_rows, kv_w), F32),
            pltpu.VMEM((LANES, kv_w), BF16),
            pltpu.VMEM((hq, 1), F32),
            pltpu.VMEM((hq, 1), F32),
            pltpu.VMEM((hq, kv_w), F32),
        ],
    )
    kernel = functools.partial(_moba_sample_kernel, gp=gp, n_grp=n_grp, n_past_blk=n_past_blk,
                               dec_seq=dec_seq, n_heads=n_heads)
    return pl.pallas_call(
        kernel,
        grid_spec=grid_spec,
        out_shape=jax.ShapeDtypeStruct((dec_batch * dec_seq, w_a), F32),
        compiler_params=_cparams(("parallel", "arbitrary", "arbitrary"), 40),
        name="moba_sample",
    )(page_table, proj, proj, proj, *([cache_k] * gp), *([cache_v] * gp))


def _tril_weight(w):
    t_i = lax.broadcasted_iota(jnp.int32, w.shape, 0)
    s_i = lax.broadcasted_iota(jnp.int32, w.shape, 1)
    return jnp.where(t_i >= s_i, w, 0.0).astype(BF16)


def _sgu_kernel(bu_ref, bv_ref, w_ref, bcol_ref, lg_ref, lb_ref, o_ref):
    rows = bu_ref.shape[0]
    n_groups = w_ref.shape[0]
    for g in range(n_groups):
        sl = slice(g * LANES, (g + 1) * LANES)
        vn = _group_ln(jax.nn.gelu(bv_ref[:, sl]), lg_ref[:, sl], lb_ref[:, sl]).astype(BF16)
        u = jax.nn.gelu(bu_ref[:, sl])
        wg = _tril_weight(w_ref[g])
        for c in range(rows // SGU_CHUNK):
            rs = slice(c * SGU_CHUNK, (c + 1) * SGU_CHUNK)
            s = _dot(wg, vn[rs]) + bcol_ref[g]
            o_ref[rs, sl] = (u[rs] * s).astype(o_ref.dtype)


def sgu_prompt(proj, w, bcol, lg, lb, layer, rows, w_b, bu_off):
    m = proj.shape[0]
    n_groups = w_b // LANES
    bu_col = bu_off // w_b
    return pl.pallas_call(
        _sgu_kernel,
        grid=(m // rows,),
        in_specs=[
            pl.BlockSpec((rows, w_b), lambda i: (i, bu_col)),
            pl.BlockSpec((rows, w_b), lambda i: (i, bu_col + 1)),
            pl.BlockSpec((None, n_groups, SGU_CHUNK, SGU_CHUNK), lambda i: (layer, 0, 0, 0)),
            pl.BlockSpec((None, n_groups, SGU_CHUNK, LANES), lambda i: (layer, 0, 0, 0)),
            pl.BlockSpec((None, 1, w_b), lambda i: (layer, 0, 0)),
            pl.BlockSpec((None, 1, w_b), lambda i: (layer, 0, 0)),
        ],
        out_specs=pl.BlockSpec((rows, w_b), lambda i: (i, 0)),
        out_shape=jax.ShapeDtypeStruct((m, w_b), BF16),
        compiler_params=_cparams(("parallel",), 32),
        name="sgu_prompt",
    )(proj, proj, w, bcol, lg, lb)


CONV_PAD = 32
CONV_ROWS = 256


def _conv_tail(y, cb, lg, lb):
    return jax.nn.silu(_group_ln(y + cb, lg, lb))


def _conv_prompt_kernel(a_ref, gt_ref, w_ref, cb_ref, lg_ref, lb_ref, o_ref, tail_ref, pad_ref):
    seq = a_ref.shape[0]
    pad_ref[0:CONV_PAD, :] = jnp.zeros((CONV_PAD, LANES), F32)
    pad_ref[CONV_PAD:CONV_PAD + seq, :] = a_ref[...] * jax.nn.sigmoid(gt_ref[...])
    tail_ref[...] = pad_ref[CONV_PAD + seq - (CONV_W - 1):CONV_PAD + seq, :]
    w = w_ref[...]
    for c0 in range(0, seq, CONV_ROWS):
        acc = jnp.zeros((CONV_ROWS, LANES), F32)
        for j in range(CONV_W):
            start = CONV_PAD + c0 + j - (CONV_W - 1)
            acc = acc + pad_ref[start:start + CONV_ROWS, :] * w[j:j + 1, :]
        o_ref[c0:c0 + CONV_ROWS, :] = _conv_tail(acc, cb_ref[...], lg_ref[...], lb_ref[...]).astype(o_ref.dtype)


def conv_prompt(proj, w, cb, lg, lb, layer, batch, seq, w_c, cg_off):
    assert seq % CONV_ROWS == 0
    n_groups = w_c // LANES
    a_col = cg_off // LANES
    vec = lambda: pl.BlockSpec((None, 1, LANES), lambda b, c: (layer, 0, c))
    return pl.pallas_call(
        _conv_prompt_kernel,
        grid=(batch, n_groups),
        in_specs=[
            pl.BlockSpec((seq, LANES), lambda b, c: (b, a_col + c)),
            pl.BlockSpec((seq, LANES), lambda b, c: (b, a_col + n_groups + c)),
            pl.BlockSpec((None, CONV_W, LANES), lambda b, c: (layer, 0, c)),
            vec(), vec(), vec(),
        ],
        out_specs=[
            pl.BlockSpec((seq, LANES), lambda b, c: (b, c)),
            pl.BlockSpec((None, CONV_W - 1, LANES), lambda b, c: (b, 0, c)),
        ],
        out_shape=[
            jax.ShapeDtypeStruct((batch * seq, w_c), BF16),
            jax.ShapeDtypeStruct((batch, CONV_W - 1, w_c), F32),
        ],
        scratch_shapes=[pltpu.VMEM((CONV_PAD + seq, LANES), F32)],
        compiler_params=_cparams(("parallel", "parallel"), 32),
        name="conv_prompt",
    )(proj, proj, w, cb, lg, lb)


def _sample_mixer_kernel(bu_ref, bv_ref, a_ref, gt_ref, st_ref, sw_ref, bcol_ref, slg_ref, slb_ref,
                         cw_ref, cb_ref, clg_ref, clb_ref,
                         sgu_ref, conv_ref, cnew_ref, vn_ref, vpad_ref, cin_ref):
    dec_seq = bu_ref.shape[0]
    n_sgu = sw_ref.shape[0]
    n_conv = cw_ref.shape[1] // LANES
    state_rows = CONV_W - 1

    vpad_ref[...] = jnp.zeros_like(vpad_ref)
    for g in range(n_sgu):
        sl = slice(g * LANES, (g + 1) * LANES)
        vn = _group_ln(jax.nn.gelu(bv_ref[:, sl]), slg_ref[:, sl], slb_ref[:, sl])
        vn_ref[:, sl] = vn
        vpad_ref[0:dec_seq, sl] = vn
    for g in range(n_sgu):
        sl = slice(g * LANES, (g + 1) * LANES)
        s = _dot(_tril_weight(sw_ref[g]), vpad_ref[:, sl].astype(BF16)) + bcol_ref[g]
        sgu_ref[:, sl] = jax.nn.gelu(bu_ref[:, sl]) * s[0:dec_seq]

    cin_ref[0:state_rows, :] = st_ref[...]
    cin_ref[state_rows:state_rows + dec_seq, :] = a_ref[...] * jax.nn.sigmoid(gt_ref[...])
    cnew_ref[...] = cin_ref[dec_seq:dec_seq + state_rows, :]
    w = cw_ref[...]
    acc = jnp.zeros(a_ref.shape, F32)
    for j in range(CONV_W):
        acc = acc + cin_ref[j:j + dec_seq, :] * w[j:j + 1, :]
    for c in range(n_conv):
        sl = slice(c * LANES, (c + 1) * LANES)
        conv_ref[:, sl] = _conv_tail(acc[:, sl], cb_ref[:, sl], clg_ref[:, sl], clb_ref[:, sl])


def sample_mixer(proj, state_conv, sw, bcol, slg, slb, cw, cb, clg, clb, layer, dec_batch, dec_seq, w_b, w_c,
                 bu_off, cg_off):
    assert dec_seq <= SGU_CHUNK and dec_seq % SUBLANES == 0
    ms = dec_batch * dec_seq
    n_sgu = w_b // LANES
    bu_col = bu_off // w_b
    a_col = cg_off // w_c
    cin_rows = -(-(CONV_W - 1 + dec_seq) // SUBLANES) * SUBLANES
    lvec = lambda width: pl.BlockSpec((None, 1, width), lambda b: (layer, 0, 0))
    return pl.pallas_call(
        _sample_mixer_kernel,
        grid=(dec_batch,),
        in_specs=[
            pl.BlockSpec((dec_seq, w_b), lambda b: (b, bu_col)),
            pl.BlockSpec((dec_seq, w_b), lambda b: (b, bu_col + 1)),
            pl.BlockSpec((dec_seq, w_c), lambda b: (b, a_col)),
            pl.BlockSpec((dec_seq, w_c), lambda b: (b, a_col + 1)),
            pl.BlockSpec((None, None, CONV_W - 1, w_c), lambda b: (layer, b, 0, 0)),
            pl.BlockSpec((None, n_sgu, SGU_CHUNK, SGU_CHUNK), lambda b: (layer, 0, 0, 0)),
            pl.BlockSpec((None, n_sgu, SGU_CHUNK, LANES), lambda b: (layer, 0, 0, 0)),
            lvec(w_b), lvec(w_b),
            pl.BlockSpec((None, CONV_W, w_c), lambda b: (layer, 0, 0)),
            lvec(w_c), lvec(w_c), lvec(w_c),
        ],
        out_specs=[
            pl.BlockSpec((dec_seq, w_b), lambda b: (b, 0)),
            pl.BlockSpec((dec_seq, w_c), lambda b: (b, 0)),
            pl.BlockSpec((None, CONV_W - 1, w_c), lambda b: (b, 0, 0)),
            pl.BlockSpec((dec_seq, w_b), lambda b: (b, 0)),
        ],
        out_shape=[
            jax.ShapeDtypeStruct((ms, w_b), F32),
            jax.ShapeDtypeStruct((ms, w_c), F32),
            jax.ShapeDtypeStruct((dec_batch, CONV_W - 1, w_c), F32),
            jax.ShapeDtypeStruct((ms, w_b), F32),
        ],
        scratch_shapes=[pltpu.VMEM((SGU_CHUNK, w_b), F32), pltpu.VMEM((cin_rows, w_c), F32)],
        compiler_params=_cparams(("parallel",), 32),
        name="sample_mixer",
    )(proj, proj, proj, proj, state_conv, sw, bcol, slg, slb, cw, cb, clg, clb)


def _out_proj_kernel(a_ref, s_ref, c_ref, wa_ref, ws_ref, wc_ref, x_ref, g_ref, o_ref):
    j = pl.program_id(1)
    tn = wa_ref.shape[-1]
    acc = (_dot(a_ref[...].astype(BF16), wa_ref[...]) + _dot(s_ref[...].astype(BF16), ws_ref[...])
           + _dot(c_ref[...].astype(BF16), wc_ref[...]))
    o_ref[:, pl.ds(pl.multiple_of(j * tn, tn), tn)] = acc

    @pl.when(j == pl.num_programs(1) - 1)
    def _():
        o_ref[...] = x_ref[...] + _rms(o_ref[...], g_ref[...])


def out_proj(attn, sgu, conv, w, x, g, layer, tm, tn):
    m, d = x.shape
    w_a, w_b, w_c = attn.shape[1], sgu.shape[1], conv.shape[1]
    assert w_a % w_b == 0 and w_b == w_c
    s_blk = w_a // w_b
    return pl.pallas_call(
        _out_proj_kernel,
        grid=(m // tm, d // tn),
        in_specs=[
            pl.BlockSpec((tm, w_a), lambda i, j: (i, 0)),
            pl.BlockSpec((tm, w_b), lambda i, j: (i, 0)),
            pl.BlockSpec((tm, w_c), lambda i, j: (i, 0)),
            pl.BlockSpec((None, w_a, tn), lambda i, j: (layer, 0, j)),
            pl.BlockSpec((None, w_b, tn), lambda i, j: (layer, s_blk, j)),
            pl.BlockSpec((None, w_c, tn), lambda i, j: (layer, s_blk + 1, j)),
            pl.BlockSpec((tm, d), lambda i, j: (i, 0)),
            pl.BlockSpec((None, 1, d), lambda i, j: (layer, 0, 0)),
        ],
        out_specs=pl.BlockSpec((tm, d), lambda i, j: (i, 0)),
        out_shape=jax.ShapeDtypeStruct((m, d), F32),
        compiler_params=_cparams(("parallel", "arbitrary"), 56),
        name=f"out_proj_m{m}",
    )(attn, sgu, conv, w, w, w, x, g)


def _ffn_kernel(x_ref, gpre_ref, wup_ref, wdn_ref, gpost_ref, o_ref, xn_ref):
    f = pl.program_id(1)

    @pl.when(f == 0)
    def _():
        xn_ref[...] = _rms(x_ref[...], gpre_ref[...]).astype(BF16)
        o_ref[...] = jnp.zeros_like(o_ref)

    h = jnp.square(jnp.maximum(_dot(xn_ref[...], wup_ref[...]), 0.0)).astype(BF16)
    o_ref[...] += _dot(h, wdn_ref[...])

    @pl.when(f == pl.num_programs(1) - 1)
    def _():
        o_ref[...] = x_ref[...] + _rms(o_ref[...], gpost_ref[...])


def ffn(x, gpre, wup, wdn, gpost, layer, tm, tf):
    m, d = x.shape
    d_ff = wup.shape[-1]
    return pl.pallas_call(
        _ffn_kernel,
        grid=(m // tm, d_ff // tf),
        in_specs=[
            pl.BlockSpec((tm, d), lambda i, f: (i, 0), pipeline_mode=pl.Buffered(1)),
            pl.BlockSpec((None, 1, d), lambda i, f: (layer, 0, 0)),
            pl.BlockSpec((None, d, tf), lambda i, f: (layer, 0, f)),
            pl.BlockSpec((None, tf, d), lambda i, f: (layer, f, 0)),
            pl.BlockSpec((None, 1, d), lambda i, f: (layer, 0, 0)),
        ],
        out_specs=pl.BlockSpec((tm, d), lambda i, f: (i, 0)),
        out_shape=jax.ShapeDtypeStruct((m, d), F32),
        scratch_shapes=[pltpu.VMEM((tm, d), BF16)],
        compiler_params=_cparams(("parallel", "arbitrary"), 58),
        name=f"ffn_m{m}",
    )(x, gpre, wup, wdn, gpost)


def _row_tile(m, target):
    t = min(m, target)
    while m % t:
        t //= 2
    return t


def kernel(x_prompt, x_sample, cache_k, cache_v, state_conv, page_table, g_mix_pre, w_in, sgu_norm_g, sgu_norm_b,
           sgu_w, sgu_b, conv_w, conv_b, conv_norm_g, conv_norm_b, w_out, g_mix_post, g_ffn_pre, w_up, w_down,
           g_ffn_post):
    batch, seq, d = x_prompt.shape
    dec_batch, dec_seq, _ = x_sample.shape
    depth = w_in.shape[0]
    w_a, w_b = d // 2, d // 4
    w_c = d - w_a - w_b
    n_kv = (w_a // HEAD_DIM) // KV_GROUP
    kv_w = n_kv * HEAD_DIM
    bu_off = w_a + 2 * kv_w
    cg_off = bu_off + 2 * w_b
    page = cache_k.shape[2]

    w_in_b, w_out_b, w_up_b, w_dn_b = (w.astype(BF16) for w in (w_in, w_out, w_up, w_down))
    vec3 = lambda v: v.reshape(depth, 1, v.shape[-1])
    g_mix_pre3, g_mix_post3, g_ffn_pre3, g_ffn_post3 = map(vec3, (g_mix_pre, g_mix_post, g_ffn_pre, g_ffn_post))
    slg3, slb3, cb3, clg3, clb3 = map(vec3, (sgu_norm_g, sgu_norm_b, conv_b, conv_norm_g, conv_norm_b))
    bcol = jnp.broadcast_to(sgu_b[..., None], sgu_b.shape + (LANES,))
    cache_k4 = cache_k.reshape(depth, cache_k.shape[1], page, kv_w)
    cache_v4 = cache_v.reshape(depth, cache_v.shape[1], page, kv_w)

    mp, ms = batch * seq, dec_batch * dec_seq
    tm_p, tm_s = _row_tile(mp, 512), _row_tile(ms, 512)
    tn = _row_tile(d, 512)
    yp = x_prompt.reshape(mp, d)
    ys = x_sample.reshape(ms, d)
    outs = [[] for _ in range(7)]
    for l in range(depth):
        proj = rms_inproj(yp, g_mix_pre3, w_in_b, l, tm_p, tn)
        attn = moba_prompt(proj, batch, seq, w_a, kv_w)
        sgu = sgu_prompt(proj, sgu_w, bcol, slg3, slb3, l, _row_tile(mp, 512), w_b, bu_off)
        conv, conv_tail = conv_prompt(proj, conv_w, cb3, clg3, clb3, l, batch, seq, w_c, cg_off)
        yp = out_proj(attn, sgu, conv, w_out_b, yp, g_mix_post3, l, tm_p, tn)
        yp = ffn(yp, g_ffn_pre3, w_up_b, w_dn_b, g_ffn_post3, l, tm_p, tn)
        outs[0].append(proj[:, w_a:w_a + kv_w].reshape(batch, seq, n_kv, HEAD_DIM))
        outs[1].append(proj[:, w_a + kv_w:bu_off].reshape(batch, seq, n_kv, HEAD_DIM))
        outs[2].append(conv_tail)
        sproj = rms_inproj(ys, g_mix_pre3, w_in_b, l, tm_s, tn)
        sattn = moba_sample(sproj, cache_k4, cache_v4, page_table, l, dec_batch, dec_seq, w_a, kv_w)
        ssgu, sconv, sconv_new, svn = sample_mixer(sproj, state_conv, sgu_w, bcol, slg3, slb3, conv_w, cb3, clg3,
                                                   clb3, l, dec_batch, dec_seq, w_b, w_c, bu_off, cg_off)
        ys = out_proj(sattn, ssgu, sconv, w_out_b, ys, g_mix_post3, l, tm_s, tn)
        ys = ffn(ys, g_ffn_pre3, w_up_b, w_dn_b, g_ffn_post3, l, tm_s, tn)
        outs[3].append(sproj[:, w_a:w_a + kv_w].reshape(dec_batch, dec_seq, n_kv, HEAD_DIM))
        outs[4].append(sproj[:, w_a + kv_w:bu_off].reshape(dec_batch, dec_seq, n_kv, HEAD_DIM))
        outs[5].append(sconv_new)
        outs[6].append(svn.reshape(dec_batch, dec_seq, w_b))
    return (yp.reshape(batch, seq, d), ys.reshape(dec_batch, dec_seq, d)) + tuple(jnp.stack(o) for o in outs)
```

```python
import functools

import jax
import jax.numpy as jnp
from jax import lax
from jax.experimental import pallas as pl
from jax.experimental.pallas import tpu as pltpu

F32 = jnp.float32
BF16 = jnp.bfloat16

HEAD_DIM = 128
KV_GROUP = 4
MOBA_BLOCK = 256
MOBA_TOPK = 3
SGU_CHUNK = 128
CONV_W = 31
RMS_EPS = 1e-6
LN_EPS = 1e-5
MASKED = -1e30
EXP2_SCALE = HEAD_DIM ** -0.5 * 1.4426950408889634
LANES = 128
SUBLANES = 8
MIB = 1024 * 1024


def _cparams(sem, vmem_mib):
    return pltpu.CompilerParams(dimension_semantics=sem, vmem_limit_bytes=vmem_mib * MIB)


def _rms(x, g):
    ms = jnp.mean(x * x, axis=-1, keepdims=True)
    return x * lax.rsqrt(ms + RMS_EPS) * g


def _group_ln(x, g, b):
    mu = jnp.mean(x, axis=-1, keepdims=True)
    d = x - mu
    var = jnp.mean(d * d, axis=-1, keepdims=True)
    return d * lax.rsqrt(var + LN_EPS) * g + b


def _dot(a, b):
    return jnp.dot(a, b, preferred_element_type=F32)


def _dot_nt(a, b):
    return lax.dot_general(a, b, (((1,), (1,)), ((), ())), preferred_element_type=F32)


def _rms_inproj_kernel(x_ref, g_ref, w_ref, o_ref, h_ref):
    @pl.when(pl.program_id(1) == 0)
    def _():
        h_ref[...] = _rms(x_ref[...], g_ref[...]).astype(BF16)

    o_ref[...] = _dot(h_ref[...], w_ref[...])


def rms_inproj(x, g, w, layer, tm, tn):
    m, d = x.shape
    n = w.shape[-1]
    assert m % tm == 0 and n % tn == 0
    return pl.pallas_call(
        _rms_inproj_kernel,
        grid=(m // tm, n // tn),
        in_specs=[
            pl.BlockSpec((tm, d), lambda i, j: (i, 0)),
            pl.BlockSpec((None, 1, d), lambda i, j: (layer, 0, 0)),
            pl.BlockSpec((None, d, tn), lambda i, j: (layer, 0, j)),
        ],
        out_specs=pl.BlockSpec((tm, tn), lambda i, j: (i, j)),
        out_shape=jax.ShapeDtypeStruct((m, n), F32),
        scratch_shapes=[pltpu.VMEM((tm, d), BF16)],
        compiler_params=_cparams(("parallel", "arbitrary"), 56),
        name=f"rms_inproj_m{m}",
    )(x, g, w)


def _moba_prompt_kernel(q_ref, k_ref, v_ref, o_ref, kb_ref, vt_ref, s_ref, p_ref, *, nb):
    blk = MOBA_BLOCK
    rows = KV_GROUP * blk
    k = k_ref[...]
    kb_ref[...] = k.astype(BF16)
    vt_ref[...] = v_ref[...].T.astype(BF16)
    nb_pad = -(-nb // SUBLANES) * SUBLANES
    k_mean = jnp.concatenate(
        [jnp.mean(k[n * blk:(n + 1) * blk], axis=0, keepdims=True) for n in range(nb)]
        + [jnp.zeros((1, HEAD_DIM), F32)] * (nb_pad - nb), axis=0).astype(BF16)
    key_i = lax.broadcasted_iota(jnp.int32, (blk, rows), 0)
    row_i = lax.broadcasted_iota(jnp.int32, (blk, rows), 1) & (blk - 1)
    causal = key_i <= row_i

    for qi in range(nb):
        qs = q_ref[qi * blk:(qi + 1) * blk, :]
        qall = jnp.concatenate(
            [qs[:, g * HEAD_DIM:(g + 1) * HEAD_DIM] for g in range(KV_GROUP)], axis=0).astype(BF16)
        n_keys = (qi + 1) * blk
        s_ref[0:n_keys, :] = _dot_nt(kb_ref[0:n_keys, :], qall)

        def block_scores(j, qi=qi):
            s = s_ref[j * blk:(j + 1) * blk, :]
            return jnp.where(causal, s, MASKED) if j == qi else s

        bmax = [jnp.max(block_scores(j), axis=0, keepdims=True) for j in range(qi + 1)]
        if qi > MOBA_TOPK:
            gate = _dot_nt(k_mean, qall)
            n_iota = lax.broadcasted_iota(jnp.int32, gate.shape, 0)
            rank = jnp.zeros(gate.shape, jnp.int32)
            for j in range(qi):
                gj = gate[j:j + 1, :]
                rank = rank + jnp.where(gj > gate, 1, jnp.where((gj == gate) & (n_iota > j), 1, 0))
            picked = [rank[j:j + 1, :] < MOBA_TOPK for j in range(qi)]
            m = bmax[qi]
            for j in range(qi):
                m = jnp.maximum(m, jnp.where(picked[j], bmax[j], MASKED))
            off = [jnp.where(picked[j], m * EXP2_SCALE, -MASKED) for j in range(qi)] + [m * EXP2_SCALE]
        else:
            m = functools.reduce(jnp.maximum, bmax)
            off = [m * EXP2_SCALE] * (qi + 1)

        l = jnp.zeros((1, rows), F32)
        for j in range(qi + 1):
            p = jnp.exp2(block_scores(j) * EXP2_SCALE - off[j])
            l = l + jnp.sum(p, axis=0, keepdims=True)
            p_ref[j * blk:(j + 1) * blk, :] = p.astype(BF16)
        out = (_dot(vt_ref[:, 0:n_keys], p_ref[0:n_keys, :]) / l).T
        for g in range(KV_GROUP):
            o_ref[qi * blk:(qi + 1) * blk, g * HEAD_DIM:(g + 1) * HEAD_DIM] = (
                out[g * blk:(g + 1) * blk, :].astype(o_ref.dtype))


def moba_prompt(proj, batch, seq, w_a, kv_w):
    assert seq % MOBA_BLOCK == 0
    nq = seq // MOBA_BLOCK
    n_kv = kv_w // HEAD_DIM
    gw = KV_GROUP * HEAD_DIM
    rows = KV_GROUP * MOBA_BLOCK
    k_col = w_a // HEAD_DIM
    v_col = (w_a + kv_w) // HEAD_DIM
    return pl.pallas_call(
        functools.partial(_moba_prompt_kernel, nb=nq),
        grid=(batch, n_kv),
        in_specs=[
            pl.BlockSpec((seq, gw), lambda b, h: (b, h)),
            pl.BlockSpec((seq, HEAD_DIM), lambda b, h: (b, k_col + h)),
            pl.BlockSpec((seq, HEAD_DIM), lambda b, h: (b, v_col + h)),
        ],
        out_specs=pl.BlockSpec((seq, gw), lambda b, h: (b, h)),
        out_shape=jax.ShapeDtypeStruct((batch * seq, w_a), BF16),
        scratch_shapes=[
            pltpu.VMEM((seq, HEAD_DIM), BF16),
            pltpu.VMEM((HEAD_DIM, seq), BF16),
            pltpu.VMEM((seq, rows), F32),
            pltpu.VMEM((seq, rows), BF16),
        ],
        compiler_params=_cparams(("parallel", "parallel"), 56),
        name="moba_prompt",
    )(proj, proj, proj)


def _moba_sample_kernel(pt_ref, q_ref, kn_ref, vn_ref, *rest, gp, n_grp, n_past_blk, dec_seq, n_heads):
    del pt_ref
    k_refs = rest[:gp]
    v_refs = rest[gp:2 * gp]
    o_ref = rest[2 * gp]
    qt_ref, s_ref, gate_ref, bmax_ref, sel_ref, m_ref, l_ref, acc_ref = rest[2 * gp + 1:]
    ph = pl.program_id(1)
    g = pl.program_id(2)
    page, n_kv = k_refs[0].shape[0], k_refs[0].shape[1]
    hq = n_heads * dec_seq
    chunk = gp * page
    blk_per_chunk = chunk // MOBA_BLOCK
    chunk_start = pl.multiple_of(g * chunk, chunk)

    def head_rows(refs):
        pages = [jnp.swapaxes(r[...], 0, 1).astype(BF16) for r in refs]
        return lambda c: jnp.concatenate([p[c] for p in pages], axis=0)

    def new_rows(ref, c):
        new = ref[:, c * HEAD_DIM:(c + 1) * HEAD_DIM]
        return jnp.concatenate([new, jnp.zeros((LANES - dec_seq, HEAD_DIM), F32)], axis=0).astype(BF16)

    def scores(key_rows):
        return sum(_dot(key_rows(c), qt_ref[c]) for c in range(n_kv))

    @pl.when((ph == 0) & (g == 0))
    def _():
        q = q_ref[...]
        zero = jnp.zeros((dec_seq, HEAD_DIM), F32)
        for c in range(n_kv):
            q_rows = jnp.concatenate(
                [q[:, h * HEAD_DIM:(h + 1) * HEAD_DIM] if h // KV_GROUP == c else zero for h in range(n_heads)], axis=0)
            qt_ref[c] = q_rows.T.astype(BF16)
        gate_ref[...] = jnp.zeros_like(gate_ref)
        bmax_ref[...] = jnp.zeros_like(bmax_ref)

    @pl.when(ph == 0)
    def _():
        s = scores(head_rows(k_refs))
        s_ref[pl.ds(chunk_start, chunk), :] = s
        for n in range(blk_per_chunk):
            sb = s[n * MOBA_BLOCK:(n + 1) * MOBA_BLOCK]
            gate_ref[pl.ds(g * blk_per_chunk + n, 1), :] = jnp.mean(sb, axis=0, keepdims=True)
            bmax_ref[pl.ds(g * blk_per_chunk + n, 1), :] = jnp.max(sb, axis=0, keepdims=True)

    @pl.when((ph == 1) & (g == 0))
    def _():
        gate = gate_ref[...]
        n_iota = lax.broadcasted_iota(jnp.int32, gate.shape, 0)
        rank = jnp.zeros(gate.shape, jnp.int32)
        for j in range(n_past_blk):
            gj = gate[j:j + 1, :]
            rank = rank + jnp.where(gj > gate, 1, jnp.where((gj == gate) & (n_iota > j), 1, 0))
        sel = (rank < MOBA_TOPK) & (n_iota < n_past_blk)
        sel_ref[...] = jnp.where(sel, 1.0, 0.0)
        m = jnp.max(jnp.where(sel, bmax_ref[...], MASKED), axis=0, keepdims=True)
        so = scores(lambda c: new_rows(kn_ref, c))
        key_t = lax.broadcasted_iota(jnp.int32, so.shape, 0)
        q_t = lax.broadcasted_iota(jnp.int32, so.shape, 1) % dec_seq
        so = jnp.where(key_t <= q_t, so, MASKED)
        m = jnp.maximum(m, jnp.max(so, axis=0, keepdims=True))
        po = jnp.exp2((so - m) * EXP2_SCALE).T
        m_ref[...] = m
        l_ref[...] = jnp.sum(po, axis=-1, keepdims=True)
        for c in range(n_kv):
            acc_ref[:, c * HEAD_DIM:(c + 1) * HEAD_DIM] = _dot(po.astype(BF16), new_rows(vn_ref, c))

    @pl.when(ph == 1)
    def _():
        p = jnp.exp2((s_ref[pl.ds(chunk_start, chunk), :] - m_ref[...]) * EXP2_SCALE)
        p = jnp.concatenate(
            [jnp.where(sel_ref[pl.ds(g * blk_per_chunk + n, 1), :] > 0.5, p[n * MOBA_BLOCK:(n + 1) * MOBA_BLOCK], 0.0)
             for n in range(blk_per_chunk)], axis=0).T
        l_ref[...] += jnp.sum(p, axis=-1, keepdims=True)
        pb = p.astype(BF16)
        v_rows = head_rows(v_refs)
        for c in range(n_kv):
            acc_ref[:, c * HEAD_DIM:(c + 1) * HEAD_DIM] += _dot(pb, v_rows(c))

    @pl.when((ph == 1) & (g == n_grp - 1))
    def _():
        out = acc_ref[...] / l_ref[...]
        for h in range(n_heads):
            c = h // KV_GROUP
            o_ref[:, h * HEAD_DIM:(h + 1) * HEAD_DIM] = out[h * dec_seq:(h + 1) * dec_seq,
                                                            c * HEAD_DIM:(c + 1) * HEAD_DIM]
    del hq


def moba_sample(proj, cache_k, cache_v, page_table, layer, dec_batch, dec_seq, w_a, kv_w):
    page, n_kv = cache_k.shape[2], cache_k.shape[3]
    n_pages = page_table.shape[1]
    past = n_pages * page
    assert past % MOBA_BLOCK == 0 and dec_seq <= MOBA_BLOCK and dec_seq % SUBLANES == 0 and dec_seq <= LANES
    n_past_blk = past // MOBA_BLOCK
    n_heads = w_a // HEAD_DIM
    hq = n_heads * dec_seq
    gp = 8
    while n_pages % gp or (gp * page) % MOBA_BLOCK:
        gp //= 2
    n_grp = n_pages // gp
    blk_rows = -(-n_past_blk // SUBLANES) * SUBLANES

    def k_map(i):
        return lambda b, ph, g, pt: (layer, pt[b, jnp.where(ph == 0, g, n_grp - 1) * gp + i], 0, 0, 0)

    def v_map(i):
        return lambda b, ph, g, pt: (layer, pt[b, jnp.where(ph == 0, 0, g) * gp + i], 0, 0, 0)

    k_col = w_a // kv_w
    page_spec = lambda index_map: pl.BlockSpec((None, None, page, n_kv, HEAD_DIM), index_map)
    grid_spec = pltpu.PrefetchScalarGridSpec(
        num_scalar_prefetch=1,
        grid=(dec_batch, 2, n_grp),
        in_specs=[
            pl.BlockSpec((dec_seq, w_a), lambda b, ph, g, pt: (b, 0)),
            pl.BlockSpec((dec_seq, kv_w), lambda b, ph, g, pt: (b, k_col)),
            pl.BlockSpec((dec_seq, kv_w), lambda b, ph, g, pt: (b, k_col + 1)),
        ] + [page_spec(k_map(i)) for i in range(gp)] + [page_spec(v_map(i)) for i in range(gp)],
        out_specs=pl.BlockSpec((dec_seq, w_a), lambda b, ph, g, pt: (b, 0)),
        scratch_shapes=[
            pltpu.VMEM((n_kv, HEAD_DIM, hq), BF16),
            pltpu.VMEM((past, hq), F32),
            pltpu.VMEM((blk_rows, hq), F32),
            pltpu.VMEM((blk_rows, hq), F32),
            pltpu.VMEM((blk_rows, hq), F32),
            pltpu.VMEM((1, hq), F32),
            pltpu.VMEM((hq, 1), F32),
            pltpu.VMEM((hq, kv_w), F32),
        ],
    )
    kernel = functools.partial(_moba_sample_kernel, gp=gp, n_grp=n_grp, n_past_blk=n_past_blk,
                               dec_seq=dec_seq, n_heads=n_heads)
    return pl.pallas_call(
        kernel,
        grid_spec=grid_spec,
        out_shape=jax.ShapeDtypeStruct((dec_batch * dec_seq, w_a), F32),
        compiler_params=_cparams(("parallel", "arbitrary", "arbitrary"), 48),
        name="moba_sample",
    )(page_table, proj, proj, proj, *([cache_k] * gp), *([cache_v] * gp))


def _tril_weight(w):
    t_i = lax.broadcasted_iota(jnp.int32, w.shape, 0)
    s_i = lax.broadcasted_iota(jnp.int32, w.shape, 1)
    return jnp.where(t_i >= s_i, w, 0.0).astype(BF16)


def _sgu_kernel(bu_ref, bv_ref, w_ref, bcol_ref, lg_ref, lb_ref, o_ref):
    rows = bu_ref.shape[0]
    n_groups = w_ref.shape[0]
    for g in range(n_groups):
        sl = slice(g * LANES, (g + 1) * LANES)
        vn = _group_ln(jax.nn.gelu(bv_ref[:, sl]), lg_ref[:, sl], lb_ref[:, sl]).astype(BF16)
        u = jax.nn.gelu(bu_ref[:, sl])
        wg = _tril_weight(w_ref[g])
        for c in range(rows // SGU_CHUNK):
            rs = slice(c * SGU_CHUNK, (c + 1) * SGU_CHUNK)
            s = _dot(wg, vn[rs]) + bcol_ref[g]
            o_ref[rs, sl] = (u[rs] * s).astype(o_ref.dtype)


def sgu_prompt(proj, w, bcol, lg, lb, layer, rows, w_b, bu_off):
    m = proj.shape[0]
    n_groups = w_b // LANES
    bu_col = bu_off // w_b
    return pl.pallas_call(
        _sgu_kernel,
        grid=(m // rows,),
        in_specs=[
            pl.BlockSpec((rows, w_b), lambda i: (i, bu_col)),
            pl.BlockSpec((rows, w_b), lambda i: (i, bu_col + 1)),
            pl.BlockSpec((None, n_groups, SGU_CHUNK, SGU_CHUNK), lambda i: (layer, 0, 0, 0)),
            pl.BlockSpec((None, n_groups, SGU_CHUNK, LANES), lambda i: (layer, 0, 0, 0)),
            pl.BlockSpec((None, 1, w_b), lambda i: (layer, 0, 0)),
            pl.BlockSpec((None, 1, w_b), lambda i: (layer, 0, 0)),
        ],
        out_specs=pl.BlockSpec((rows, w_b), lambda i: (i, 0)),
        out_shape=jax.ShapeDtypeStruct((m, w_b), BF16),
        compiler_params=_cparams(("parallel",), 32),
        name="sgu_prompt",
    )(proj, proj, w, bcol, lg, lb)


CONV_PAD = 32
CONV_ROWS = 256


def _conv_tail(y, cb, lg, lb):
    return jax.nn.silu(_group_ln(y + cb, lg, lb))


def _conv_prompt_kernel(a_ref, gt_ref, w_ref, cb_ref, lg_ref, lb_ref, o_ref, tail_ref, pad_ref):
    seq = a_ref.shape[0]
    pad_ref[0:CONV_PAD, :] = jnp.zeros((CONV_PAD, LANES), F32)
    pad_ref[CONV_PAD:CONV_PAD + seq, :] = a_ref[...] * jax.nn.sigmoid(gt_ref[...])
    tail_ref[...] = pad_ref[CONV_PAD + seq - (CONV_W - 1):CONV_PAD + seq, :]
    w = w_ref[...]
    for c0 in range(0, seq, CONV_ROWS):
        acc = jnp.zeros((CONV_ROWS, LANES), F32)
        for j in range(CONV_W):
            start = CONV_PAD + c0 + j - (CONV_W - 1)
            acc = acc + pad_ref[start:start + CONV_ROWS, :] * w[j:j + 1, :]
        o_ref[c0:c0 + CONV_ROWS, :] = _conv_tail(acc, cb_ref[...], lg_ref[...], lb_ref[...]).astype(o_ref.dtype)


def conv_prompt(proj, w, cb, lg, lb, layer, batch, seq, w_c, cg_off):
    assert seq % CONV_ROWS == 0
    n_groups = w_c // LANES
    a_col = cg_off // LANES
    vec = lambda: pl.BlockSpec((None, 1, LANES), lambda b, c: (layer, 0, c))
    return pl.pallas_call(
        _conv_prompt_kernel,
        grid=(batch, n_groups),
        in_specs=[
            pl.BlockSpec((seq, LANES), lambda b, c: (b, a_col + c)),
            pl.BlockSpec((seq, LANES), lambda b, c: (b, a_col + n_groups + c)),
            pl.BlockSpec((None, CONV_W, LANES), lambda b, c: (layer, 0, c)),
            vec(), vec(), vec(),
        ],
        out_specs=[
            pl.BlockSpec((seq, LANES), lambda b, c: (b, c)),
            pl.BlockSpec((None, CONV_W - 1, LANES), lambda b, c: (b, 0, c)),
        ],
        out_shape=[
            jax.ShapeDtypeStruct((batch * seq, w_c), BF16),
            jax.ShapeDtypeStruct((batch, CONV_W - 1, w_c), F32),
        ],
        scratch_shapes=[pltpu.VMEM((CONV_PAD + seq, LANES), F32)],
        compiler_params=_cparams(("parallel", "parallel"), 32),
        name="conv_prompt",
    )(proj, proj, w, cb, lg, lb)


def _sample_mixer_kernel(bu_ref, bv_ref, a_ref, gt_ref, st_ref, sw_ref, bcol_ref, slg_ref, slb_ref,
                         cw_ref, cb_ref, clg_ref, clb_ref,
                         sgu_ref, conv_ref, cnew_ref, vn_ref, vpad_ref, cin_ref):
    dec_seq = bu_ref.shape[0]
    n_sgu = sw_ref.shape[0]
    n_conv = cw_ref.shape[1] // LANES
    state_rows = CONV_W - 1

    vpad_ref[...] = jnp.zeros_like(vpad_ref)
    for g in range(n_sgu):
        sl = slice(g * LANES, (g + 1) * LANES)
        vn = _group_ln(jax.nn.gelu(bv_ref[:, sl]), slg_ref[:, sl], slb_ref[:, sl])
        vn_ref[:, sl] = vn
        vpad_ref[0:dec_seq, sl] = vn
    for g in range(n_sgu):
        sl = slice(g * LANES, (g + 1) * LANES)
        s = _dot(_tril_weight(sw_ref[g]), vpad_ref[:, sl].astype(BF16)) + bcol_ref[g]
        sgu_ref[:, sl] = jax.nn.gelu(bu_ref[:, sl]) * s[0:dec_seq]

    cin_ref[0:state_rows, :] = st_ref[...]
    cin_ref[state_rows:state_rows + dec_seq, :] = a_ref[...] * jax.nn.sigmoid(gt_ref[...])
    cnew_ref[...] = cin_ref[dec_seq:dec_seq + state_rows, :]
    w = cw_ref[...]
    acc = jnp.zeros(a_ref.shape, F32)
    for j in range(CONV_W):
        acc = acc + cin_ref[j:j + dec_seq, :] * w[j:j + 1, :]
    for c in range(n_conv):
        sl = slice(c * LANES, (c + 1) * LANES)
        conv_ref[:, sl] = _conv_tail(acc[:, sl], cb_ref[:, sl], clg_ref[:, sl], clb_ref[:, sl])


def sample_mixer(proj, state_conv, sw, bcol, slg, slb, cw, cb, clg, clb, layer, dec_batch, dec_seq, w_b, w_c,
                 bu_off, cg_off):
    assert dec_seq <= SGU_CHUNK and dec_seq % SUBLANES == 0
    ms = dec_batch * dec_seq
    n_sgu = w_b // LANES
    bu_col = bu_off // w_b
    a_col = cg_off // w_c
    cin_rows = -(-(CONV_W - 1 + dec_seq) // SUBLANES) * SUBLANES
    lvec = lambda width: pl.BlockSpec((None, 1, width), lambda b: (layer, 0, 0))
    return pl.pallas_call(
        _sample_mixer_kernel,
        grid=(dec_batch,),
        in_specs=[
            pl.BlockSpec((dec_seq, w_b), lambda b: (b, bu_col)),
            pl.BlockSpec((dec_seq, w_b), lambda b: (b, bu_col + 1)),
            pl.BlockSpec((dec_seq, w_c), lambda b: (b, a_col)),
            pl.BlockSpec((dec_seq, w_c), lambda b: (b, a_col + 1)),
            pl.BlockSpec((None, None, CONV_W - 1, w_c), lambda b: (layer, b, 0, 0)),
            pl.BlockSpec((None, n_sgu, SGU_CHUNK, SGU_CHUNK), lambda b: (layer, 0, 0, 0)),
            pl.BlockSpec((None, n_sgu, SGU_CHUNK, LANES), lambda b: (layer, 0, 0, 0)),
            lvec(w_b), lvec(w_b),
            pl.BlockSpec((None, CONV_W, w_c), lambda b: (layer, 0, 0)),
            lvec(w_c), lvec(w_c), lvec(w_c),
        ],
        out_specs=[
            pl.BlockSpec((dec_seq, w_b), lambda b: (b, 0)),
            pl.BlockSpec((dec_seq, w_c), lambda b: (b, 0)),
            pl.BlockSpec((None, CONV_W - 1, w_c), lambda b: (b, 0, 0)),
            pl.BlockSpec((dec_seq, w_b), lambda b: (b, 0)),
        ],
        out_shape=[
            jax.ShapeDtypeStruct((ms, w_b), F32),
            jax.ShapeDtypeStruct((ms, w_c), F32),
            jax.ShapeDtypeStruct((dec_batch, CONV_W - 1, w_c), F32),
            jax.ShapeDtypeStruct((ms, w_b), F32),
        ],
        scratch_shapes=[pltpu.VMEM((SGU_CHUNK, w_b), F32), pltpu.VMEM((cin_rows, w_c), F32)],
        compiler_params=_cparams(("parallel",), 32),
        name="sample_mixer",
    )(proj, proj, proj, proj, state_conv, sw, bcol, slg, slb, cw, cb, clg, clb)


def _out_proj_kernel(a_ref, s_ref, c_ref, wa_ref, ws_ref, wc_ref, x_ref, g_ref, o_ref):
    j = pl.program_id(1)
    tn = wa_ref.shape[-1]
    acc = (_dot(a_ref[...].astype(BF16), wa_ref[...]) + _dot(s_ref[...].astype(BF16), ws_ref[...])
           + _dot(c_ref[...].astype(BF16), wc_ref[...]))
    o_ref[:, pl.ds(pl.multiple_of(j * tn, tn), tn)] = acc

    @pl.when(j == pl.num_programs(1) - 1)
    def _():
        o_ref[...] = x_ref[...] + _rms(o_ref[...], g_ref[...])


def out_proj(attn, sgu, conv, w, x, g, layer, tm, tn):
    m, d = x.shape
    w_a, w_b, w_c = attn.shape[1], sgu.shape[1], conv.shape[1]
    assert w_a % w_b == 0 and w_b == w_c and m % tm == 0 and d % tn == 0
    s_blk = w_a // w_b
    return pl.pallas_call(
        _out_proj_kernel,
        grid=(m // tm, d // tn),
        in_specs=[
            pl.BlockSpec((tm, w_a), lambda i, j: (i, 0)),
            pl.BlockSpec((tm, w_b), lambda i, j: (i, 0)),
            pl.BlockSpec((tm, w_c), lambda i, j: (i, 0)),
            pl.BlockSpec((None, w_a, tn), lambda i, j: (layer, 0, j)),
            pl.BlockSpec((None, w_b, tn), lambda i, j: (layer, s_blk, j)),
            pl.BlockSpec((None, w_c, tn), lambda i, j: (layer, s_blk + 1, j)),
            pl.BlockSpec((tm, d), lambda i, j: (i, 0), pipeline_mode=pl.Buffered(1)),
            pl.BlockSpec((None, 1, d), lambda i, j: (layer, 0, 0)),
        ],
        out_specs=pl.BlockSpec((tm, d), lambda i, j: (i, 0)),
        out_shape=jax.ShapeDtypeStruct((m, d), F32),
        compiler_params=_cparams(("parallel", "arbitrary"), 58),
        name=f"out_proj_m{m}",
    )(attn, sgu, conv, w, w, w, x, g)


def _ffn_kernel(x_ref, gpre_ref, wup_ref, wdn_ref, gpost_ref, o_ref, xn_ref):
    f = pl.program_id(1)

    @pl.when(f == 0)
    def _():
        xn_ref[...] = _rms(x_ref[...], gpre_ref[...]).astype(BF16)
        o_ref[...] = jnp.zeros_like(o_ref)

    h = jnp.square(jnp.maximum(_dot(xn_ref[...], wup_ref[...]), 0.0)).astype(BF16)
    o_ref[...] += _dot(h, wdn_ref[...])

    @pl.when(f == pl.num_programs(1) - 1)
    def _():
        o_ref[...] = x_ref[...] + _rms(o_ref[...], gpost_ref[...])


def ffn(x, gpre, wup, wdn, gpost, layer, tm, tf):
    m, d = x.shape
    d_ff = wup.shape[-1]
    assert m % tm == 0 and d_ff % tf == 0
    return pl.pallas_call(
        _ffn_kernel,
        grid=(m // tm, d_ff // tf),
        in_specs=[
            pl.BlockSpec((tm, d), lambda i, f: (i, 0), pipeline_mode=pl.Buffered(1)),
            pl.BlockSpec((None, 1, d), lambda i, f: (layer, 0, 0)),
            pl.BlockSpec((None, d, tf), lambda i, f: (layer, 0, f)),
            pl.BlockSpec((None, tf, d), lambda i, f: (layer, f, 0)),
            pl.BlockSpec((None, 1, d), lambda i, f: (layer, 0, 0)),
        ],
        out_specs=pl.BlockSpec((tm, d), lambda i, f: (i, 0)),
        out_shape=jax.ShapeDtypeStruct((m, d), F32),
        scratch_shapes=[pltpu.VMEM((tm, d), BF16)],
        compiler_params=_cparams(("parallel", "arbitrary"), 58),
        name=f"ffn_m{m}",
    )(x, gpre, wup, wdn, gpost)


def _row_tile(m, target):
    t = min(m, target)
    while m % t:
        t //= 2
    return t


def kernel(x_prompt, x_sample, cache_k, cache_v, state_conv, page_table, g_mix_pre, w_in, sgu_norm_g, sgu_norm_b,
           sgu_w, sgu_b, conv_w, conv_b, conv_norm_g, conv_norm_b, w_out, g_mix_post, g_ffn_pre, w_up, w_down,
           g_ffn_post):
    batch, seq, d = x_prompt.shape
    dec_batch, dec_seq, _ = x_sample.shape
    depth = w_in.shape[0]
    w_a, w_b = d // 2, d // 4
    w_c = d - w_a - w_b
    n_kv = (w_a // HEAD_DIM) // KV_GROUP
    kv_w = n_kv * HEAD_DIM
    bu_off = w_a + 2 * kv_w
    cg_off = bu_off + 2 * w_b

    w_in_b, w_out_b, w_up_b, w_dn_b = (w.astype(BF16) for w in (w_in, w_out, w_up, w_down))
    vec3 = lambda v: v.reshape(depth, 1, v.shape[-1])
    g_mix_pre3, g_mix_post3, g_ffn_pre3, g_ffn_post3 = map(vec3, (g_mix_pre, g_mix_post, g_ffn_pre, g_ffn_post))
    slg3, slb3, cb3, clg3, clb3 = map(vec3, (sgu_norm_g, sgu_norm_b, conv_b, conv_norm_g, conv_norm_b))
    bcol = jnp.broadcast_to(sgu_b[..., None], sgu_b.shape + (LANES,))

    mp, ms = batch * seq, dec_batch * dec_seq
    tm_p, tm_s = _row_tile(mp, 512), _row_tile(ms, 512)
    tn = _row_tile(d, 512)
    yp = x_prompt.reshape(mp, d)
    ys = x_sample.reshape(ms, d)
    outs = [[] for _ in range(7)]
    for l in range(depth):
        proj = rms_inproj(yp, g_mix_pre3, w_in_b, l, tm_p, _row_tile(w_in.shape[-1], 1024))
        attn = moba_prompt(proj, batch, seq, w_a, kv_w)
        sgu = sgu_prompt(proj, sgu_w, bcol, slg3, slb3, l, _row_tile(mp, 512), w_b, bu_off)
        conv, conv_tail = conv_prompt(proj, conv_w, cb3, clg3, clb3, l, batch, seq, w_c, cg_off)
        yp = out_proj(attn, sgu, conv, w_out_b, yp, g_mix_post3, l, tm_p, tn)
        yp = ffn(yp, g_ffn_pre3, w_up_b, w_dn_b, g_ffn_post3, l, tm_p, tn)
        outs[0].append(proj[:, w_a:w_a + kv_w].reshape(batch, seq, n_kv, HEAD_DIM))
        outs[1].append(proj[:, w_a + kv_w:bu_off].reshape(batch, seq, n_kv, HEAD_DIM))
        outs[2].append(conv_tail)
        sproj = rms_inproj(ys, g_mix_pre3, w_in_b, l, tm_s, tn)
        sattn = moba_sample(sproj, cache_k, cache_v, page_table, l, dec_batch, dec_seq, w_a, kv_w)
        ssgu, sconv, sconv_new, svn = sample_mixer(sproj, state_conv, sgu_w, bcol, slg3, slb3, conv_w, cb3, clg3,
                                                   clb3, l, dec_batch, dec_seq, w_b, w_c, bu_off, cg_off)
        ys = out_proj(sattn, ssgu, sconv, w_out_b, ys, g_mix_post3, l, tm_s, tn)
        ys = ffn(ys, g_ffn_pre3, w_up_b, w_dn_b, g_ffn_post3, l, tm_s, tn)
        outs[3].append(sproj[:, w_a:w_a + kv_w].reshape(dec_batch, dec_seq, n_kv, HEAD_DIM))
        outs[4].append(sproj[:, w_a + kv_w:bu_off].reshape(dec_batch, dec_seq, n_kv, HEAD_DIM))
        outs[5].append(sconv_new)
        outs[6].append(svn.reshape(dec_batch, dec_seq, w_b))
    return (yp.reshape(batch, seq, d), ys.reshape(dec_batch, dec_seq, d)) + tuple(jnp.stack(o) for o in outs)
```

```python
import functools

import jax
import jax.numpy as jnp
from jax import lax
from jax.experimental import pallas as pl
from jax.experimental.pallas import tpu as pltpu

F32 = jnp.float32
BF16 = jnp.bfloat16

HEAD_DIM = 128
KV_GROUP = 4
MOBA_BLOCK = 256
MOBA_TOPK = 3
SGU_CHUNK = 128
CONV_W = 31
RMS_EPS = 1e-6
LN_EPS = 1e-5
MASKED = -1e30
EXP2_SCALE = HEAD_DIM ** -0.5 * 1.4426950408889634
LANES = 128
SUBLANES = 8
MIB = 1024 * 1024


def _cparams(sem, vmem_mib):
    return pltpu.CompilerParams(dimension_semantics=sem, vmem_limit_bytes=vmem_mib * MIB)


def _rms(x, g):
    ms = jnp.mean(x * x, axis=-1, keepdims=True)
    return x * lax.rsqrt(ms + RMS_EPS) * g


def _group_ln(x, g, b):
    mu = jnp.mean(x, axis=-1, keepdims=True)
    d = x - mu
    var = jnp.mean(d * d, axis=-1, keepdims=True)
    return d * lax.rsqrt(var + LN_EPS) * g + b


def _dot(a, b):
    return jnp.dot(a, b, preferred_element_type=F32)


def _dot_nt(a, b):
    return lax.dot_general(a, b, (((1,), (1,)), ((), ())), preferred_element_type=F32)


def _bf16_weight(w_ref, copy_refs):
    w = w_ref[...]
    if copy_refs:
        w = w.astype(BF16)
        copy_refs[0][...] = w
    return w


def _weight_copy_out(w, block, index_map):
    if w.dtype == BF16:
        return [], []
    return [pl.BlockSpec((None,) + block, index_map)], [jax.ShapeDtypeStruct((1,) + w.shape[1:], BF16)]


def _rms_inproj_kernel(x_ref, g_ref, w_ref, o_ref, *rest):
    *w_copy, h_ref = rest

    @pl.when(pl.program_id(1) == 0)
    def _():
        h_ref[...] = _rms(x_ref[...], g_ref[...]).astype(BF16)

    o_ref[...] = _dot(h_ref[...], _bf16_weight(w_ref, w_copy))


def rms_inproj(x, g, w, layer, w_layer, tm, tn):
    m, d = x.shape
    n = w.shape[-1]
    assert m % tm == 0 and n % tn == 0 and (w.dtype == BF16 or m == tm)
    copy_specs, copy_shapes = _weight_copy_out(w, (d, tn), lambda i, j: (0, 0, j))
    out = pl.pallas_call(
        _rms_inproj_kernel,
        grid=(m // tm, n // tn),
        in_specs=[
            pl.BlockSpec((tm, d), lambda i, j: (i, 0)),
            pl.BlockSpec((None, 1, d), lambda i, j: (layer, 0, 0)),
            pl.BlockSpec((None, d, tn), lambda i, j: (w_layer, 0, j)),
        ],
        out_specs=[pl.BlockSpec((tm, tn), lambda i, j: (i, j))] + copy_specs,
        out_shape=[jax.ShapeDtypeStruct((m, n), F32)] + copy_shapes,
        scratch_shapes=[pltpu.VMEM((tm, d), BF16)],
        compiler_params=_cparams(("parallel", "arbitrary"), 56),
        name=f"rms_inproj_m{m}",
    )(x, g, w)
    return out if copy_specs else out[0]


def _moba_prompt_kernel(q_ref, k_ref, v_ref, o_ref, kb_ref, vt_ref, s_ref, p_ref, *, nb):
    blk = MOBA_BLOCK
    rows = KV_GROUP * blk
    k = k_ref[...]
    kb_ref[...] = k.astype(BF16)
    vt_ref[...] = v_ref[...].T.astype(BF16)
    nb_pad = -(-nb // SUBLANES) * SUBLANES
    k_mean = jnp.concatenate(
        [jnp.mean(k[n * blk:(n + 1) * blk], axis=0, keepdims=True) for n in range(nb)]
        + [jnp.zeros((1, HEAD_DIM), F32)] * (nb_pad - nb), axis=0).astype(BF16)
    key_i = lax.broadcasted_iota(jnp.int32, (blk, rows), 0)
    row_i = lax.broadcasted_iota(jnp.int32, (blk, rows), 1) & (blk - 1)
    causal = key_i <= row_i

    for qi in range(nb):
        qs = q_ref[qi * blk:(qi + 1) * blk, :]
        qall = jnp.concatenate(
            [qs[:, g * HEAD_DIM:(g + 1) * HEAD_DIM] for g in range(KV_GROUP)], axis=0).astype(BF16)
        n_keys = (qi + 1) * blk
        s_ref[0:n_keys, :] = _dot_nt(kb_ref[0:n_keys, :], qall)

        def block_scores(j, qi=qi):
            s = s_ref[j * blk:(j + 1) * blk, :]
            return jnp.where(causal, s, MASKED) if j == qi else s

        bmax = [jnp.max(block_scores(j), axis=0, keepdims=True) for j in range(qi + 1)]
        if qi > MOBA_TOPK:
            gate = _dot_nt(k_mean, qall)
            n_iota = lax.broadcasted_iota(jnp.int32, gate.shape, 0)
            rank = jnp.zeros(gate.shape, jnp.int32)
            for j in range(qi):
                gj = gate[j:j + 1, :]
                rank = rank + jnp.where(gj > gate, 1, jnp.where((gj == gate) & (n_iota > j), 1, 0))
            picked = [rank[j:j + 1, :] < MOBA_TOPK for j in range(qi)]
            m = bmax[qi]
            for j in range(qi):
                m = jnp.maximum(m, jnp.where(picked[j], bmax[j], MASKED))
            off = [jnp.where(picked[j], m * EXP2_SCALE, -MASKED) for j in range(qi)] + [m * EXP2_SCALE]
        else:
            m = functools.reduce(jnp.maximum, bmax)
            off = [m * EXP2_SCALE] * (qi + 1)

        l = jnp.zeros((1, rows), F32)
        for j in range(qi + 1):
            p = jnp.exp2(block_scores(j) * EXP2_SCALE - off[j])
            l = l + jnp.sum(p, axis=0, keepdims=True)
            p_ref[j * blk:(j + 1) * blk, :] = p.astype(BF16)
        out = (_dot(vt_ref[:, 0:n_keys], p_ref[0:n_keys, :]) / l).T
        for g in range(KV_GROUP):
            o_ref[qi * blk:(qi + 1) * blk, g * HEAD_DIM:(g + 1) * HEAD_DIM] = (
                out[g * blk:(g + 1) * blk, :].astype(o_ref.dtype))


def moba_prompt(proj, batch, seq, w_a, kv_w):
    assert seq % MOBA_BLOCK == 0
    nq = seq // MOBA_BLOCK
    n_kv = kv_w // HEAD_DIM
    gw = KV_GROUP * HEAD_DIM
    rows = KV_GROUP * MOBA_BLOCK
    k_col = w_a // HEAD_DIM
    v_col = (w_a + kv_w) // HEAD_DIM
    return pl.pallas_call(
        functools.partial(_moba_prompt_kernel, nb=nq),
        grid=(batch, n_kv),
        in_specs=[
            pl.BlockSpec((seq, gw), lambda b, h: (b, h)),
            pl.BlockSpec((seq, HEAD_DIM), lambda b, h: (b, k_col + h)),
            pl.BlockSpec((seq, HEAD_DIM), lambda b, h: (b, v_col + h)),
        ],
        out_specs=pl.BlockSpec((seq, gw), lambda b, h: (b, h)),
        out_shape=jax.ShapeDtypeStruct((batch * seq, w_a), BF16),
        scratch_shapes=[
            pltpu.VMEM((seq, HEAD_DIM), BF16),
            pltpu.VMEM((HEAD_DIM, seq), BF16),
            pltpu.VMEM((seq, rows), F32),
            pltpu.VMEM((seq, rows), BF16),
        ],
        compiler_params=_cparams(("parallel", "parallel"), 56),
        name="moba_prompt",
    )(proj, proj, proj)


def _moba_sample_kernel(pt_ref, q_ref, kn_ref, vn_ref, *rest, gp, n_grp, n_past_blk, dec_seq, n_heads):
    del pt_ref
    k_refs = rest[:gp]
    v_refs = rest[gp:2 * gp]
    o_ref = rest[2 * gp]
    qt_ref, s_ref, gate_ref, bmax_ref, sel_ref, m_ref, l_ref, acc_ref = rest[2 * gp + 1:]
    ph = pl.program_id(1)
    g = pl.program_id(2)
    page, n_kv = k_refs[0].shape[0], k_refs[0].shape[1]
    hq = n_heads * dec_seq
    chunk = gp * page
    blk_per_chunk = chunk // MOBA_BLOCK
    chunk_start = pl.multiple_of(g * chunk, chunk)

    def head_rows(refs):
        pages = [jnp.swapaxes(r[...], 0, 1).astype(BF16) for r in refs]
        return lambda c: jnp.concatenate([p[c] for p in pages], axis=0)

    def new_rows(ref, c):
        new = ref[:, c * HEAD_DIM:(c + 1) * HEAD_DIM]
        return jnp.concatenate([new, jnp.zeros((LANES - dec_seq, HEAD_DIM), F32)], axis=0).astype(BF16)

    def scores(key_rows):
        return sum(_dot(key_rows(c), qt_ref[c]) for c in range(n_kv))

    @pl.when((ph == 0) & (g == 0))
    def _():
        q = q_ref[...]
        zero = jnp.zeros((dec_seq, HEAD_DIM), F32)
        for c in range(n_kv):
            q_rows = jnp.concatenate(
                [q[:, h * HEAD_DIM:(h + 1) * HEAD_DIM] if h // KV_GROUP == c else zero for h in range(n_heads)], axis=0)
            qt_ref[c] = q_rows.T.astype(BF16)
        gate_ref[...] = jnp.zeros_like(gate_ref)
        bmax_ref[...] = jnp.zeros_like(bmax_ref)

    @pl.when(ph == 0)
    def _():
        s = scores(head_rows(k_refs))
        s_ref[pl.ds(chunk_start, chunk), :] = s
        for n in range(blk_per_chunk):
            sb = s[n * MOBA_BLOCK:(n + 1) * MOBA_BLOCK]
            gate_ref[pl.ds(g * blk_per_chunk + n, 1), :] = jnp.mean(sb, axis=0, keepdims=True)
            bmax_ref[pl.ds(g * blk_per_chunk + n, 1), :] = jnp.max(sb, axis=0, keepdims=True)

    @pl.when((ph == 1) & (g == 0))
    def _():
        gate = gate_ref[...]
        n_iota = lax.broadcasted_iota(jnp.int32, gate.shape, 0)
        rank = jnp.zeros(gate.shape, jnp.int32)
        for j in range(n_past_blk):
            gj = gate[j:j + 1, :]
            rank = rank + jnp.where(gj > gate, 1, jnp.where((gj == gate) & (n_iota > j), 1, 0))
        sel = (rank < MOBA_TOPK) & (n_iota < n_past_blk)
        sel_ref[...] = jnp.where(sel, 1.0, 0.0)
        m = jnp.max(jnp.where(sel, bmax_ref[...], MASKED), axis=0, keepdims=True)
        so = scores(lambda c: new_rows(kn_ref, c))
        key_t = lax.broadcasted_iota(jnp.int32, so.shape, 0)
        q_t = lax.broadcasted_iota(jnp.int32, so.shape, 1) % dec_seq
        so = jnp.where(key_t <= q_t, so, MASKED)
        m = jnp.maximum(m, jnp.max(so, axis=0, keepdims=True))
        po = jnp.exp2((so - m) * EXP2_SCALE).T
        m_ref[...] = m
        l_ref[...] = jnp.sum(po, axis=-1, keepdims=True)
        for c in range(n_kv):
            acc_ref[:, c * HEAD_DIM:(c + 1) * HEAD_DIM] = _dot(po.astype(BF16), new_rows(vn_ref, c))

    @pl.when(ph == 1)
    def _():
        p = jnp.exp2((s_ref[pl.ds(chunk_start, chunk), :] - m_ref[...]) * EXP2_SCALE)
        p = jnp.concatenate(
            [jnp.where(sel_ref[pl.ds(g * blk_per_chunk + n, 1), :] > 0.5, p[n * MOBA_BLOCK:(n + 1) * MOBA_BLOCK], 0.0)
             for n in range(blk_per_chunk)], axis=0).T
        l_ref[...] += jnp.sum(p, axis=-1, keepdims=True)
        pb = p.astype(BF16)
        v_rows = head_rows(v_refs)
        for c in range(n_kv):
            acc_ref[:, c * HEAD_DIM:(c + 1) * HEAD_DIM] += _dot(pb, v_rows(c))

    @pl.when((ph == 1) & (g == n_grp - 1))
    def _():
        out = acc_ref[...] / l_ref[...]
        for h in range(n_heads):
            c = h // KV_GROUP
            o_ref[:, h * HEAD_DIM:(h + 1) * HEAD_DIM] = out[h * dec_seq:(h + 1) * dec_seq,
                                                            c * HEAD_DIM:(c + 1) * HEAD_DIM]
    del hq


def moba_sample(proj, cache_k, cache_v, page_table, layer, dec_batch, dec_seq, w_a, kv_w):
    page, n_kv = cache_k.shape[2], cache_k.shape[3]
    n_pages = page_table.shape[1]
    past = n_pages * page
    assert past % MOBA_BLOCK == 0 and dec_seq <= MOBA_BLOCK and dec_seq % SUBLANES == 0 and dec_seq <= LANES
    n_past_blk = past // MOBA_BLOCK
    n_heads = w_a // HEAD_DIM
    hq = n_heads * dec_seq
    gp = 8
    while n_pages % gp or (gp * page) % MOBA_BLOCK:
        gp //= 2
    n_grp = n_pages // gp
    blk_rows = -(-n_past_blk // SUBLANES) * SUBLANES

    def k_map(i):
        return lambda b, ph, g, pt: (layer, pt[b, jnp.where(ph == 0, g, n_grp - 1) * gp + i], 0, 0, 0)

    def v_map(i):
        return lambda b, ph, g, pt: (layer, pt[b, jnp.where(ph == 0, 0, g) * gp + i], 0, 0, 0)

    k_col = w_a // kv_w
    page_spec = lambda index_map: pl.BlockSpec((None, None, page, n_kv, HEAD_DIM), index_map)
    grid_spec = pltpu.PrefetchScalarGridSpec(
        num_scalar_prefetch=1,
        grid=(dec_batch, 2, n_grp),
        in_specs=[
            pl.BlockSpec((dec_seq, w_a), lambda b, ph, g, pt: (b, 0)),
            pl.BlockSpec((dec_seq, kv_w), lambda b, ph, g, pt: (b, k_col)),
            pl.BlockSpec((dec_seq, kv_w), lambda b, ph, g, pt: (b, k_col + 1)),
        ] + [page_spec(k_map(i)) for i in range(gp)] + [page_spec(v_map(i)) for i in range(gp)],
        out_specs=pl.BlockSpec((dec_seq, w_a), lambda b, ph, g, pt: (b, 0)),
        scratch_shapes=[
            pltpu.VMEM((n_kv, HEAD_DIM, hq), BF16),
            pltpu.VMEM((past, hq), F32),
            pltpu.VMEM((blk_rows, hq), F32),
            pltpu.VMEM((blk_rows, hq), F32),
            pltpu.VMEM((blk_rows, hq), F32),
            pltpu.VMEM((1, hq), F32),
            pltpu.VMEM((hq, 1), F32),
            pltpu.VMEM((hq, kv_w), F32),
        ],
    )
    kernel = functools.partial(_moba_sample_kernel, gp=gp, n_grp=n_grp, n_past_blk=n_past_blk,
                               dec_seq=dec_seq, n_heads=n_heads)
    return pl.pallas_call(
        kernel,
        grid_spec=grid_spec,
        out_shape=jax.ShapeDtypeStruct((dec_batch * dec_seq, w_a), F32),
        compiler_params=_cparams(("parallel", "arbitrary", "arbitrary"), 48),
        name="moba_sample",
    )(page_table, proj, proj, proj, *([cache_k] * gp), *([cache_v] * gp))


def _tril_weight(w):
    t_i = lax.broadcasted_iota(jnp.int32, w.shape, 0)
    s_i = lax.broadcasted_iota(jnp.int32, w.shape, 1)
    return jnp.where(t_i >= s_i, w, 0.0).astype(BF16)


def _sgu_kernel(bu_ref, bv_ref, w_ref, bcol_ref, lg_ref, lb_ref, o_ref):
    rows = bu_ref.shape[0]
    n_groups = w_ref.shape[0]
    for g in range(n_groups):
        sl = slice(g * LANES, (g + 1) * LANES)
        vn = _group_ln(jax.nn.gelu(bv_ref[:, sl]), lg_ref[:, sl], lb_ref[:, sl]).astype(BF16)
        u = jax.nn.gelu(bu_ref[:, sl])
        wg = _tril_weight(w_ref[g])
        for c in range(rows // SGU_CHUNK):
            rs = slice(c * SGU_CHUNK, (c + 1) * SGU_CHUNK)
            s = _dot(wg, vn[rs]) + bcol_ref[g]
            o_ref[rs, sl] = (u[rs] * s).astype(o_ref.dtype)


def sgu_prompt(proj, w, bcol, lg, lb, layer, rows, w_b, bu_off):
    m = proj.shape[0]
    n_groups = w_b // LANES
    bu_col = bu_off // w_b
    return pl.pallas_call(
        _sgu_kernel,
        grid=(m // rows,),
        in_specs=[
            pl.BlockSpec((rows, w_b), lambda i: (i, bu_col)),
            pl.BlockSpec((rows, w_b), lambda i: (i, bu_col + 1)),
            pl.BlockSpec((None, n_groups, SGU_CHUNK, SGU_CHUNK), lambda i: (layer, 0, 0, 0)),
            pl.BlockSpec((None, n_groups, SGU_CHUNK, LANES), lambda i: (layer, 0, 0, 0)),
            pl.BlockSpec((None, 1, w_b), lambda i: (layer, 0, 0)),
            pl.BlockSpec((None, 1, w_b), lambda i: (layer, 0, 0)),
        ],
        out_specs=pl.BlockSpec((rows, w_b), lambda i: (i, 0)),
        out_shape=jax.ShapeDtypeStruct((m, w_b), BF16),
        compiler_params=_cparams(("parallel",), 32),
        name="sgu_prompt",
    )(proj, proj, w, bcol, lg, lb)


CONV_PAD = 32
CONV_ROWS = 256


def _conv_tail(y, cb, lg, lb):
    return jax.nn.silu(_group_ln(y + cb, lg, lb))


def _conv_prompt_kernel(a_ref, gt_ref, w_ref, cb_ref, lg_ref, lb_ref, o_ref, tail_ref, pad_ref):
    seq = a_ref.shape[0]
    pad_ref[0:CONV_PAD, :] = jnp.zeros((CONV_PAD, LANES), F32)
    pad_ref[CONV_PAD:CONV_PAD + seq, :] = a_ref[...] * jax.nn.sigmoid(gt_ref[...])
    tail_ref[...] = pad_ref[CONV_PAD + seq - (CONV_W - 1):CONV_PAD + seq, :]
    w = w_ref[...]
    for c0 in range(0, seq, CONV_ROWS):
        acc = jnp.zeros((CONV_ROWS, LANES), F32)
        for j in range(CONV_W):
            start = CONV_PAD + c0 + j - (CONV_W - 1)
            acc = acc + pad_ref[start:start + CONV_ROWS, :] * w[j:j + 1, :]
        o_ref[c0:c0 + CONV_ROWS, :] = _conv_tail(acc, cb_ref[...], lg_ref[...], lb_ref[...]).astype(o_ref.dtype)


def conv_prompt(proj, w, cb, lg, lb, layer, batch, seq, w_c, cg_off):
    assert seq % CONV_ROWS == 0
    n_groups = w_c // LANES
    a_col = cg_off // LANES
    vec = lambda: pl.BlockSpec((None, 1, LANES), lambda b, c: (layer, 0, c))
    return pl.pallas_call(
        _conv_prompt_kernel,
        grid=(batch, n_groups),
        in_specs=[
            pl.BlockSpec((seq, LANES), lambda b, c: (b, a_col + c)),
            pl.BlockSpec((seq, LANES), lambda b, c: (b, a_col + n_groups + c)),
            pl.BlockSpec((None, CONV_W, LANES), lambda b, c: (layer, 0, c)),
            vec(), vec(), vec(),
        ],
        out_specs=[
            pl.BlockSpec((seq, LANES), lambda b, c: (b, c)),
            pl.BlockSpec((None, CONV_W - 1, LANES), lambda b, c: (b, 0, c)),
        ],
        out_shape=[
            jax.ShapeDtypeStruct((batch * seq, w_c), BF16),
            jax.ShapeDtypeStruct((batch, CONV_W - 1, w_c), F32),
        ],
        scratch_shapes=[pltpu.VMEM((CONV_PAD + seq, LANES), F32)],
        compiler_params=_cparams(("parallel", "parallel"), 32),
        name="conv_prompt",
    )(proj, proj, w, cb, lg, lb)


def _sample_mixer_kernel(bu_ref, bv_ref, a_ref, gt_ref, st_ref, sw_ref, bcol_ref, slg_ref, slb_ref,
                         cw_ref, cb_ref, clg_ref, clb_ref,
                         sgu_ref, conv_ref, cnew_ref, vn_ref, vpad_ref, cin_ref):
    dec_seq = bu_ref.shape[0]
    n_sgu = sw_ref.shape[0]
    n_conv = cw_ref.shape[1] // LANES
    state_rows = CONV_W - 1

    vpad_ref[...] = jnp.zeros_like(vpad_ref)
    for g in range(n_sgu):
        sl = slice(g * LANES, (g + 1) * LANES)
        vn = _group_ln(jax.nn.gelu(bv_ref[:, sl]), slg_ref[:, sl], slb_ref[:, sl])
        vn_ref[:, sl] = vn
        vpad_ref[0:dec_seq, sl] = vn
    for g in range(n_sgu):
        sl = slice(g * LANES, (g + 1) * LANES)
        s = _dot(_tril_weight(sw_ref[g]), vpad_ref[:, sl].astype(BF16)) + bcol_ref[g]
        sgu_ref[:, sl] = jax.nn.gelu(bu_ref[:, sl]) * s[0:dec_seq]

    cin_ref[0:state_rows, :] = st_ref[...]
    cin_ref[state_rows:state_rows + dec_seq, :] = a_ref[...] * jax.nn.sigmoid(gt_ref[...])
    cnew_ref[...] = cin_ref[dec_seq:dec_seq + state_rows, :]
    w = cw_ref[...]
    acc = jnp.zeros(a_ref.shape, F32)
    for j in range(CONV_W):
        acc = acc + cin_ref[j:j + dec_seq, :] * w[j:j + 1, :]
    for c in range(n_conv):
        sl = slice(c * LANES, (c + 1) * LANES)
        conv_ref[:, sl] = _conv_tail(acc[:, sl], cb_ref[:, sl], clg_ref[:, sl], clb_ref[:, sl])


def sample_mixer(proj, state_conv, sw, bcol, slg, slb, cw, cb, clg, clb, layer, dec_batch, dec_seq, w_b, w_c,
                 bu_off, cg_off):
    assert dec_seq <= SGU_CHUNK and dec_seq % SUBLANES == 0
    ms = dec_batch * dec_seq
    n_sgu = w_b // LANES
    bu_col = bu_off // w_b
    a_col = cg_off // w_c
    cin_rows = -(-(CONV_W - 1 + dec_seq) // SUBLANES) * SUBLANES
    lvec = lambda width: pl.BlockSpec((None, 1, width), lambda b: (layer, 0, 0))
    return pl.pallas_call(
        _sample_mixer_kernel,
        grid=(dec_batch,),
        in_specs=[
            pl.BlockSpec((dec_seq, w_b), lambda b: (b, bu_col)),
            pl.BlockSpec((dec_seq, w_b), lambda b: (b, bu_col + 1)),
            pl.BlockSpec((dec_seq, w_c), lambda b: (b, a_col)),
            pl.BlockSpec((dec_seq, w_c), lambda b: (b, a_col + 1)),
            pl.BlockSpec((None, None, CONV_W - 1, w_c), lambda b: (layer, b, 0, 0)),
            pl.BlockSpec((None, n_sgu, SGU_CHUNK, SGU_CHUNK), lambda b: (layer, 0, 0, 0)),
            pl.BlockSpec((None, n_sgu, SGU_CHUNK, LANES), lambda b: (layer, 0, 0, 0)),
            lvec(w_b), lvec(w_b),
            pl.BlockSpec((None, CONV_W, w_c), lambda b: (layer, 0, 0)),
            lvec(w_c), lvec(w_c), lvec(w_c),
        ],
        out_specs=[
            pl.BlockSpec((dec_seq, w_b), lambda b: (b, 0)),
            pl.BlockSpec((dec_seq, w_c), lambda b: (b, 0)),
            pl.BlockSpec((None, CONV_W - 1, w_c), lambda b: (b, 0, 0)),
            pl.BlockSpec((dec_seq, w_b), lambda b: (b, 0)),
        ],
        out_shape=[
            jax.ShapeDtypeStruct((ms, w_b), F32),
            jax.ShapeDtypeStruct((ms, w_c), F32),
            jax.ShapeDtypeStruct((dec_batch, CONV_W - 1, w_c), F32),
            jax.ShapeDtypeStruct((ms, w_b), F32),
        ],
        scratch_shapes=[pltpu.VMEM((SGU_CHUNK, w_b), F32), pltpu.VMEM((cin_rows, w_c), F32)],
        compiler_params=_cparams(("parallel",), 32),
        name="sample_mixer",
    )(proj, proj, proj, proj, state_conv, sw, bcol, slg, slb, cw, cb, clg, clb)


def _out_proj_kernel(a_ref, s_ref, c_ref, w_ref, x_ref, g_ref, o_ref, *w_copy):
    j = pl.program_id(1)
    tn = w_ref.shape[-1]
    w_a, w_b = a_ref.shape[1], s_ref.shape[1]
    w = _bf16_weight(w_ref, w_copy)
    acc = (_dot(a_ref[...].astype(BF16), w[0:w_a]) + _dot(s_ref[...].astype(BF16), w[w_a:w_a + w_b])
           + _dot(c_ref[...].astype(BF16), w[w_a + w_b:]))
    o_ref[:, pl.ds(pl.multiple_of(j * tn, tn), tn)] = acc

    @pl.when(j == pl.num_programs(1) - 1)
    def _():
        o_ref[...] = x_ref[...] + _rms(o_ref[...], g_ref[...])


def out_proj(attn, sgu, conv, w, x, g, layer, w_layer, tm, tn):
    m, d = x.shape
    w_a, w_b, w_c = attn.shape[1], sgu.shape[1], conv.shape[1]
    assert w_a + w_b + w_c == w.shape[1] and m % tm == 0 and d % tn == 0 and (w.dtype == BF16 or m == tm)
    copy_specs, copy_shapes = _weight_copy_out(w, (w.shape[1], tn), lambda i, j: (0, 0, j))
    out = pl.pallas_call(
        _out_proj_kernel,
        grid=(m // tm, d // tn),
        in_specs=[
            pl.BlockSpec((tm, w_a), lambda i, j: (i, 0)),
            pl.BlockSpec((tm, w_b), lambda i, j: (i, 0)),
            pl.BlockSpec((tm, w_c), lambda i, j: (i, 0)),
            pl.BlockSpec((None, w.shape[1], tn), lambda i, j: (w_layer, 0, j)),
            pl.BlockSpec((tm, d), lambda i, j: (i, 0)),
            pl.BlockSpec((None, 1, d), lambda i, j: (layer, 0, 0)),
        ],
        out_specs=[pl.BlockSpec((tm, d), lambda i, j: (i, 0))] + copy_specs,
        out_shape=[jax.ShapeDtypeStruct((m, d), F32)] + copy_shapes,
        compiler_params=_cparams(("parallel", "arbitrary"), 58),
        name=f"out_proj_m{m}",
    )(attn, sgu, conv, w, x, g)
    return out if copy_specs else out[0]


def _ffn_kernel(x_ref, gpre_ref, wup_ref, wdn_ref, gpost_ref, o_ref, *rest):
    *w_copies, xn_ref = rest
    f = pl.program_id(1)

    @pl.when(f == 0)
    def _():
        xn_ref[...] = _rms(x_ref[...], gpre_ref[...]).astype(BF16)
        o_ref[...] = jnp.zeros_like(o_ref)

    h = jnp.square(jnp.maximum(_dot(xn_ref[...], _bf16_weight(wup_ref, w_copies[:1])), 0.0)).astype(BF16)
    o_ref[...] += _dot(h, _bf16_weight(wdn_ref, w_copies[1:]))

    @pl.when(f == pl.num_programs(1) - 1)
    def _():
        o_ref[...] = x_ref[...] + _rms(o_ref[...], gpost_ref[...])


def ffn(x, gpre, wup, wdn, gpost, layer, w_layer, tm, tf):
    m, d = x.shape
    d_ff = wup.shape[-1]
    assert m % tm == 0 and d_ff % tf == 0 and wup.dtype == wdn.dtype and (wup.dtype == BF16 or m == tm)
    up_specs, up_shapes = _weight_copy_out(wup, (d, tf), lambda i, f: (0, 0, f))
    dn_specs, dn_shapes = _weight_copy_out(wdn, (tf, d), lambda i, f: (0, f, 0))
    out = pl.pallas_call(
        _ffn_kernel,
        grid=(m // tm, d_ff // tf),
        in_specs=[
            pl.BlockSpec((tm, d), lambda i, f: (i, 0), pipeline_mode=pl.Buffered(1)),
            pl.BlockSpec((None, 1, d), lambda i, f: (layer, 0, 0)),
            pl.BlockSpec((None, d, tf), lambda i, f: (w_layer, 0, f)),
            pl.BlockSpec((None, tf, d), lambda i, f: (w_layer, f, 0)),
            pl.BlockSpec((None, 1, d), lambda i, f: (layer, 0, 0)),
        ],
        out_specs=[pl.BlockSpec((tm, d), lambda i, f: (i, 0))] + up_specs + dn_specs,
        out_shape=[jax.ShapeDtypeStruct((m, d), F32)] + up_shapes + dn_shapes,
        scratch_shapes=[pltpu.VMEM((tm, d), BF16)],
        compiler_params=_cparams(("parallel", "arbitrary"), 58),
        name=f"ffn_m{m}",
    )(x, gpre, wup, wdn, gpost)
    return out if up_specs else out[0]


def _row_tile(m, target):
    t = min(m, target)
    while m % t:
        t //= 2
    return t


def kernel(x_prompt, x_sample, cache_k, cache_v, state_conv, page_table, g_mix_pre, w_in, sgu_norm_g, sgu_norm_b,
           sgu_w, sgu_b, conv_w, conv_b, conv_norm_g, conv_norm_b, w_out, g_mix_post, g_ffn_pre, w_up, w_down,
           g_ffn_post):
    batch, seq, d = x_prompt.shape
    dec_batch, dec_seq, _ = x_sample.shape
    depth = w_in.shape[0]
    w_a, w_b = d // 2, d // 4
    w_c = d - w_a - w_b
    n_kv = (w_a // HEAD_DIM) // KV_GROUP
    kv_w = n_kv * HEAD_DIM
    bu_off = w_a + 2 * kv_w
    cg_off = bu_off + 2 * w_b

    vec3 = lambda v: v.reshape(depth, 1, v.shape[-1])
    g_mix_pre3, g_mix_post3, g_ffn_pre3, g_ffn_post3 = map(vec3, (g_mix_pre, g_mix_post, g_ffn_pre, g_ffn_post))
    slg3, slb3, cb3, clg3, clb3 = map(vec3, (sgu_norm_g, sgu_norm_b, conv_b, conv_norm_g, conv_norm_b))
    bcol = jnp.broadcast_to(sgu_b[..., None], sgu_b.shape + (LANES,))

    mp, ms = batch * seq, dec_batch * dec_seq
    tm_p, tm_s = _row_tile(mp, 512), _row_tile(ms, 512)
    tn = _row_tile(d, 512)
    yp = x_prompt.reshape(mp, d)
    ys = x_sample.reshape(ms, d)
    cast_in_decode = ms == tm_s
    if not cast_in_decode:
        w_in, w_out, w_up, w_down = (w.astype(BF16) for w in (w_in, w_out, w_up, w_down))
    tf_s = _row_tile(w_up.shape[-1], 256) if cast_in_decode else tn

    def split(result, stacked, layer):
        return (result[0], result[1:], 0) if cast_in_decode else (result, stacked, layer)

    outs = [[] for _ in range(7)]
    for l in range(depth):
        sproj, (w_in_b,), wl = split(rms_inproj(ys, g_mix_pre3, w_in, l, l, tm_s, tn), (w_in,), l)
        sattn = moba_sample(sproj, cache_k, cache_v, page_table, l, dec_batch, dec_seq, w_a, kv_w)
        ssgu, sconv, sconv_new, svn = sample_mixer(sproj, state_conv, sgu_w, bcol, slg3, slb3, conv_w, cb3, clg3,
                                                   clb3, l, dec_batch, dec_seq, w_b, w_c, bu_off, cg_off)
        ys, (w_out_b,), _ = split(out_proj(sattn, ssgu, sconv, w_out, ys, g_mix_post3, l, l, tm_s, tn), (w_out,), l)
        ys, (w_up_b, w_dn_b), _ = split(ffn(ys, g_ffn_pre3, w_up, w_down, g_ffn_post3, l, l, tm_s, tf_s),
                                        (w_up, w_down), l)
        outs[3].append(sproj[:, w_a:w_a + kv_w].reshape(dec_batch, dec_seq, n_kv, HEAD_DIM))
        outs[4].append(sproj[:, w_a + kv_w:bu_off].reshape(dec_batch, dec_seq, n_kv, HEAD_DIM))
        outs[5].append(sconv_new)
        outs[6].append(svn.reshape(dec_batch, dec_seq, w_b))
        proj = rms_inproj(yp, g_mix_pre3, w_in_b, l, wl, tm_p, _row_tile(w_in.shape[-1], 1024))
        attn = moba_prompt(proj, batch, seq, w_a, kv_w)
        sgu = sgu_prompt(proj, sgu_w, bcol, slg3, slb3, l, _row_tile(mp, 512), w_b, bu_off)
        conv, conv_tail = conv_prompt(proj, conv_w, cb3, clg3, clb3, l, batch, seq, w_c, cg_off)
        yp = out_proj(attn, sgu, conv, w_out_b, yp, g_mix_post3, l, wl, tm_p, tn)
        yp = ffn(yp, g_ffn_pre3, w_up_b, w_dn_b, g_ffn_post3, l, wl, tm_p, tn)
        outs[0].append(proj[:, w_a:w_a + kv_w].reshape(batch, seq, n_kv, HEAD_DIM))
        outs[1].append(proj[:, w_a + kv_w:bu_off].reshape(batch, seq, n_kv, HEAD_DIM))
        outs[2].append(conv_tail)
    return (yp.reshape(batch, seq, d), ys.reshape(dec_batch, dec_seq, d)) + tuple(jnp.stack(o) for o in outs)
```

```python
import functools

import jax
import jax.numpy as jnp
from jax import lax
from jax.experimental import pallas as pl
from jax.experimental.pallas import tpu as pltpu

F32 = jnp.float32
BF16 = jnp.bfloat16

HEAD_DIM = 128
KV_GROUP = 4
MOBA_BLOCK = 256
MOBA_TOPK = 3
SGU_CHUNK = 128
CONV_W = 31
RMS_EPS = 1e-6
LN_EPS = 1e-5
MASKED = -1e30
EXP2_SCALE = HEAD_DIM ** -0.5 * 1.4426950408889634
LANES = 128
SUBLANES = 8
MIB = 1024 * 1024


def _cparams(sem, vmem_mib):
    return pltpu.CompilerParams(dimension_semantics=sem, vmem_limit_bytes=vmem_mib * MIB)


def _rms(x, g):
    ms = jnp.mean(x * x, axis=-1, keepdims=True)
    return x * lax.rsqrt(ms + RMS_EPS) * g


def _group_ln(x, g, b):
    mu = jnp.mean(x, axis=-1, keepdims=True)
    d = x - mu
    var = jnp.mean(d * d, axis=-1, keepdims=True)
    return d * lax.rsqrt(var + LN_EPS) * g + b


def _dot(a, b):
    return jnp.dot(a, b, preferred_element_type=F32)


def _dot_nt(a, b):
    return lax.dot_general(a, b, (((1,), (1,)), ((), ())), preferred_element_type=F32)


def _bf16_weight(w_ref, copy_refs):
    w = w_ref[...]
    if copy_refs:
        w = w.astype(BF16)
        copy_refs[0][...] = w
    return w


def _weight_copy_out(w, block, index_map):
    if w.dtype == BF16:
        return [], []
    return [pl.BlockSpec((None,) + block, index_map)], [jax.ShapeDtypeStruct((1,) + w.shape[1:], BF16)]


def _rms_inproj_kernel(x_ref, g_ref, w_ref, o_ref, *rest):
    *w_copy, h_ref = rest

    @pl.when(pl.program_id(1) == 0)
    def _():
        h_ref[...] = _rms(x_ref[...], g_ref[...]).astype(BF16)

    o_ref[...] = _dot(h_ref[...], _bf16_weight(w_ref, w_copy))


def rms_inproj(x, g, w, layer, w_layer, tm, tn):
    m, d = x.shape
    n = w.shape[-1]
    assert m % tm == 0 and n % tn == 0 and (w.dtype == BF16 or m == tm)
    copy_specs, copy_shapes = _weight_copy_out(w, (d, tn), lambda i, j: (0, 0, j))
    out = pl.pallas_call(
        _rms_inproj_kernel,
        grid=(m // tm, n // tn),
        in_specs=[
            pl.BlockSpec((tm, d), lambda i, j: (i, 0)),
            pl.BlockSpec((None, 1, d), lambda i, j: (layer, 0, 0)),
            pl.BlockSpec((None, d, tn), lambda i, j: (w_layer, 0, j)),
        ],
        out_specs=[pl.BlockSpec((tm, tn), lambda i, j: (i, j))] + copy_specs,
        out_shape=[jax.ShapeDtypeStruct((m, n), F32)] + copy_shapes,
        scratch_shapes=[pltpu.VMEM((tm, d), BF16)],
        compiler_params=_cparams(("parallel", "arbitrary"), 56),
        name=f"rms_inproj_m{m}",
    )(x, g, w)
    return out if copy_specs else out[0]


def _moba_prompt_kernel(q_ref, k_ref, v_ref, o_ref, kb_ref, vt_ref, s_ref, p_ref, *, nb):
    blk = MOBA_BLOCK
    rows = KV_GROUP * blk
    k = k_ref[...]
    kb_ref[...] = k.astype(BF16)
    vt_ref[...] = v_ref[...].T.astype(BF16)
    nb_pad = -(-nb // SUBLANES) * SUBLANES
    k_mean = jnp.concatenate(
        [jnp.mean(k[n * blk:(n + 1) * blk], axis=0, keepdims=True) for n in range(nb)]
        + [jnp.zeros((1, HEAD_DIM), F32)] * (nb_pad - nb), axis=0).astype(BF16)
    key_i = lax.broadcasted_iota(jnp.int32, (blk, rows), 0)
    row_i = lax.broadcasted_iota(jnp.int32, (blk, rows), 1) & (blk - 1)
    causal = key_i <= row_i

    for qi in range(nb):
        qs = q_ref[qi * blk:(qi + 1) * blk, :]
        qall = jnp.concatenate(
            [qs[:, g * HEAD_DIM:(g + 1) * HEAD_DIM] for g in range(KV_GROUP)], axis=0).astype(BF16)
        n_keys = (qi + 1) * blk
        s_ref[0:n_keys, :] = _dot_nt(kb_ref[0:n_keys, :], qall)

        def block_scores(j, qi=qi):
            s = s_ref[j * blk:(j + 1) * blk, :]
            return jnp.where(causal, s, MASKED) if j == qi else s

        bmax = [jnp.max(block_scores(j), axis=0, keepdims=True) for j in range(qi + 1)]
        if qi > MOBA_TOPK:
            gate = _dot_nt(k_mean, qall)
            n_iota = lax.broadcasted_iota(jnp.int32, gate.shape, 0)
            rank = jnp.zeros(gate.shape, jnp.int32)
            for j in range(qi):
                gj = gate[j:j + 1, :]
                rank = rank + jnp.where(gj > gate, 1, jnp.where((gj == gate) & (n_iota > j), 1, 0))
            picked = [rank[j:j + 1, :] < MOBA_TOPK for j in range(qi)]
            m = bmax[qi]
            for j in range(qi):
                m = jnp.maximum(m, jnp.where(picked[j], bmax[j], MASKED))
            off = [jnp.where(picked[j], m * EXP2_SCALE, -MASKED) for j in range(qi)] + [m * EXP2_SCALE]
        else:
            m = functools.reduce(jnp.maximum, bmax)
            off = [m * EXP2_SCALE] * (qi + 1)

        l = jnp.zeros((1, rows), F32)
        for j in range(qi + 1):
            p = jnp.exp2(block_scores(j) * EXP2_SCALE - off[j])
            l = l + jnp.sum(p, axis=0, keepdims=True)
            p_ref[j * blk:(j + 1) * blk, :] = p.astype(BF16)
        out = (_dot(vt_ref[:, 0:n_keys], p_ref[0:n_keys, :]) / l).T
        for g in range(KV_GROUP):
            o_ref[qi * blk:(qi + 1) * blk, g * HEAD_DIM:(g + 1) * HEAD_DIM] = (
                out[g * blk:(g + 1) * blk, :].astype(o_ref.dtype))


def moba_prompt(proj, batch, seq, w_a, kv_w):
    assert seq % MOBA_BLOCK == 0
    nq = seq // MOBA_BLOCK
    n_kv = kv_w // HEAD_DIM
    gw = KV_GROUP * HEAD_DIM
    rows = KV_GROUP * MOBA_BLOCK
    k_col = w_a // HEAD_DIM
    v_col = (w_a + kv_w) // HEAD_DIM
    return pl.pallas_call(
        functools.partial(_moba_prompt_kernel, nb=nq),
        grid=(batch, n_kv),
        in_specs=[
            pl.BlockSpec((seq, gw), lambda b, h: (b, h)),
            pl.BlockSpec((seq, HEAD_DIM), lambda b, h: (b, k_col + h)),
            pl.BlockSpec((seq, HEAD_DIM), lambda b, h: (b, v_col + h)),
        ],
        out_specs=pl.BlockSpec((seq, gw), lambda b, h: (b, h)),
        out_shape=jax.ShapeDtypeStruct((batch * seq, w_a), BF16),
        scratch_shapes=[
            pltpu.VMEM((seq, HEAD_DIM), BF16),
            pltpu.VMEM((HEAD_DIM, seq), BF16),
            pltpu.VMEM((seq, rows), F32),
            pltpu.VMEM((seq, rows), BF16),
        ],
        compiler_params=_cparams(("parallel", "parallel"), 56),
        name="moba_prompt",
    )(proj, proj, proj)


def _moba_sample_kernel(pt_ref, q_ref, kn_ref, vn_ref, ck_ref, cv_ref, o_ref,
                        qt_ref, s_ref, gate_ref, bmax_ref, sel_ref, m_ref, l_ref, acc_ref, buf_ref, sem,
                        *, layer, gp, n_grp, n_batch, n_past_blk, dec_seq, n_heads):
    ph = pl.program_id(1)
    g = pl.program_id(2)
    page, n_kv = buf_ref.shape[2], buf_ref.shape[3]
    chunk = gp * page
    blk_per_chunk = chunk // MOBA_BLOCK
    chunk_start = pl.multiple_of(g * chunk, chunk)
    steps_per_batch = 2 * n_grp
    n_steps = n_batch * steps_per_batch
    step = pl.program_id(0) * steps_per_batch + ph * n_grp + g
    slot = lax.rem(step, 2)

    def for_page_copies(of_step, act):
        b_of = lax.div(of_step, steps_per_batch)
        in_batch = lax.rem(of_step, steps_per_batch)
        ph_of, g_of = lax.div(in_batch, n_grp), lax.rem(in_batch, n_grp)
        to_slot = lax.rem(of_step, 2)
        for cache_ref, cache_pass in ((ck_ref, 0), (cv_ref, 1)):
            @pl.when(ph_of == cache_pass)
            def _(cache_ref=cache_ref):
                for i in range(gp):
                    page_id = pt_ref[b_of, g_of * gp + i]
                    act(pltpu.make_async_copy(cache_ref.at[layer, page_id], buf_ref.at[to_slot, i], sem.at[to_slot]))

    @pl.when(step == 0)
    def _():
        for_page_copies(step, lambda copy: copy.start())

    @pl.when(step + 1 < n_steps)
    def _():
        for_page_copies(step + 1, lambda copy: copy.start())

    for_page_copies(step, lambda copy: copy.wait())

    def head_rows():
        pages = [jnp.swapaxes(buf_ref[slot, i], 0, 1).astype(BF16) for i in range(gp)]
        return lambda c: jnp.concatenate([p[c] for p in pages], axis=0)

    def new_rows(ref, c):
        new = ref[:, c * HEAD_DIM:(c + 1) * HEAD_DIM]
        return jnp.concatenate([new, jnp.zeros((LANES - dec_seq, HEAD_DIM), F32)], axis=0).astype(BF16)

    def scores(key_rows):
        return sum(_dot(key_rows(c), qt_ref[c]) for c in range(n_kv))

    @pl.when((ph == 0) & (g == 0))
    def _():
        q = q_ref[...]
        zero = jnp.zeros((dec_seq, HEAD_DIM), F32)
        for c in range(n_kv):
            q_rows = jnp.concatenate(
                [q[:, h * HEAD_DIM:(h + 1) * HEAD_DIM] if h // KV_GROUP == c else zero for h in range(n_heads)], axis=0)
            qt_ref[c] = q_rows.T.astype(BF16)
        gate_ref[...] = jnp.zeros_like(gate_ref)
        bmax_ref[...] = jnp.zeros_like(bmax_ref)

    @pl.when(ph == 0)
    def _():
        s = scores(head_rows())
        s_ref[pl.ds(chunk_start, chunk), :] = s
        for n in range(blk_per_chunk):
            sb = s[n * MOBA_BLOCK:(n + 1) * MOBA_BLOCK]
            gate_ref[pl.ds(g * blk_per_chunk + n, 1), :] = jnp.mean(sb, axis=0, keepdims=True)
            bmax_ref[pl.ds(g * blk_per_chunk + n, 1), :] = jnp.max(sb, axis=0, keepdims=True)

    @pl.when((ph == 1) & (g == 0))
    def _():
        gate = gate_ref[...]
        n_iota = lax.broadcasted_iota(jnp.int32, gate.shape, 0)
        rank = jnp.zeros(gate.shape, jnp.int32)
        for j in range(n_past_blk):
            gj = gate[j:j + 1, :]
            rank = rank + jnp.where(gj > gate, 1, jnp.where((gj == gate) & (n_iota > j), 1, 0))
        sel = (rank < MOBA_TOPK) & (n_iota < n_past_blk)
        sel_ref[...] = jnp.where(sel, 1.0, 0.0)
        m = jnp.max(jnp.where(sel, bmax_ref[...], MASKED), axis=0, keepdims=True)
        so = scores(lambda c: new_rows(kn_ref, c))
        key_t = lax.broadcasted_iota(jnp.int32, so.shape, 0)
        q_t = lax.broadcasted_iota(jnp.int32, so.shape, 1) % dec_seq
        so = jnp.where(key_t <= q_t, so, MASKED)
        m = jnp.maximum(m, jnp.max(so, axis=0, keepdims=True))
        po = jnp.exp2((so - m) * EXP2_SCALE).T
        m_ref[...] = m
        l_ref[...] = jnp.sum(po, axis=-1, keepdims=True)
        for c in range(n_kv):
            acc_ref[:, c * HEAD_DIM:(c + 1) * HEAD_DIM] = _dot(po.astype(BF16), new_rows(vn_ref, c))

    @pl.when(ph == 1)
    def _():
        p = jnp.exp2((s_ref[pl.ds(chunk_start, chunk), :] - m_ref[...]) * EXP2_SCALE)
        p = jnp.concatenate(
            [jnp.where(sel_ref[pl.ds(g * blk_per_chunk + n, 1), :] > 0.5, p[n * MOBA_BLOCK:(n + 1) * MOBA_BLOCK], 0.0)
             for n in range(blk_per_chunk)], axis=0).T
        l_ref[...] += jnp.sum(p, axis=-1, keepdims=True)
        pb = p.astype(BF16)
        v_rows = head_rows()
        for c in range(n_kv):
            acc_ref[:, c * HEAD_DIM:(c + 1) * HEAD_DIM] += _dot(pb, v_rows(c))

    @pl.when((ph == 1) & (g == n_grp - 1))
    def _():
        out = acc_ref[...] / l_ref[...]
        for h in range(n_heads):
            c = h // KV_GROUP
            o_ref[:, h * HEAD_DIM:(h + 1) * HEAD_DIM] = out[h * dec_seq:(h + 1) * dec_seq,
                                                            c * HEAD_DIM:(c + 1) * HEAD_DIM]


def moba_sample(proj, cache_k, cache_v, page_table, layer, dec_batch, dec_seq, w_a, kv_w):
    page, n_kv = cache_k.shape[2], cache_k.shape[3]
    n_pages = page_table.shape[1]
    past = n_pages * page
    assert past % MOBA_BLOCK == 0 and dec_seq <= MOBA_BLOCK and dec_seq % SUBLANES == 0 and dec_seq <= LANES
    n_past_blk = past // MOBA_BLOCK
    n_heads = w_a // HEAD_DIM
    hq = n_heads * dec_seq
    gp = 8
    while n_pages % gp or (gp * page) % MOBA_BLOCK:
        gp //= 2
    n_grp = n_pages // gp
    blk_rows = -(-n_past_blk // SUBLANES) * SUBLANES
    k_col = w_a // kv_w
    grid_spec = pltpu.PrefetchScalarGridSpec(
        num_scalar_prefetch=1,
        grid=(dec_batch, 2, n_grp),
        in_specs=[
            pl.BlockSpec((dec_seq, w_a), lambda b, ph, g, pt: (b, 0)),
            pl.BlockSpec((dec_seq, kv_w), lambda b, ph, g, pt: (b, k_col)),
            pl.BlockSpec((dec_seq, kv_w), lambda b, ph, g, pt: (b, k_col + 1)),
            pl.BlockSpec(memory_space=pl.ANY),
            pl.BlockSpec(memory_space=pl.ANY),
        ],
        out_specs=pl.BlockSpec((dec_seq, w_a), lambda b, ph, g, pt: (b, 0)),
        scratch_shapes=[
            pltpu.VMEM((n_kv, HEAD_DIM, hq), BF16),
            pltpu.VMEM((past, hq), F32),
            pltpu.VMEM((blk_rows, hq), F32),
            pltpu.VMEM((blk_rows, hq), F32),
            pltpu.VMEM((blk_rows, hq), F32),
            pltpu.VMEM((1, hq), F32),
            pltpu.VMEM((hq, 1), F32),
            pltpu.VMEM((hq, kv_w), F32),
            pltpu.VMEM((2, gp, page, n_kv, HEAD_DIM), F32),
            pltpu.SemaphoreType.DMA((2,)),
        ],
    )
    kernel = functools.partial(_moba_sample_kernel, layer=layer, gp=gp, n_grp=n_grp, n_batch=dec_batch,
                               n_past_blk=n_past_blk, dec_seq=dec_seq, n_heads=n_heads)
    return pl.pallas_call(
        kernel,
        grid_spec=grid_spec,
        out_shape=jax.ShapeDtypeStruct((dec_batch * dec_seq, w_a), F32),
        compiler_params=_cparams(("arbitrary", "arbitrary", "arbitrary"), 48),
        name="moba_sample",
    )(page_table, proj, proj, proj, cache_k, cache_v)


def _tril_weight(w):
    t_i = lax.broadcasted_iota(jnp.int32, w.shape, 0)
    s_i = lax.broadcasted_iota(jnp.int32, w.shape, 1)
    return jnp.where(t_i >= s_i, w, 0.0).astype(BF16)


def _sgu_kernel(bu_ref, bv_ref, w_ref, bcol_ref, lg_ref, lb_ref, o_ref):
    rows = bu_ref.shape[0]
    n_groups = w_ref.shape[0]
    for g in range(n_groups):
        sl = slice(g * LANES, (g + 1) * LANES)
        vn = _group_ln(jax.nn.gelu(bv_ref[:, sl]), lg_ref[:, sl], lb_ref[:, sl]).astype(BF16)
        u = jax.nn.gelu(bu_ref[:, sl])
        wg = _tril_weight(w_ref[g])
        for c in range(rows // SGU_CHUNK):
            rs = slice(c * SGU_CHUNK, (c + 1) * SGU_CHUNK)
            s = _dot(wg, vn[rs]) + bcol_ref[g]
            o_ref[rs, sl] = (u[rs] * s).astype(o_ref.dtype)


def sgu_prompt(proj, w, bcol, lg, lb, layer, rows, w_b, bu_off):
    m = proj.shape[0]
    n_groups = w_b // LANES
    bu_col = bu_off // w_b
    return pl.pallas_call(
        _sgu_kernel,
        grid=(m // rows,),
        in_specs=[
            pl.BlockSpec((rows, w_b), lambda i: (i, bu_col)),
            pl.BlockSpec((rows, w_b), lambda i: (i, bu_col + 1)),
            pl.BlockSpec((None, n_groups, SGU_CHUNK, SGU_CHUNK), lambda i: (layer, 0, 0, 0)),
            pl.BlockSpec((None, n_groups, SGU_CHUNK, LANES), lambda i: (layer, 0, 0, 0)),
            pl.BlockSpec((None, 1, w_b), lambda i: (layer, 0, 0)),
            pl.BlockSpec((None, 1, w_b), lambda i: (layer, 0, 0)),
        ],
        out_specs=pl.BlockSpec((rows, w_b), lambda i: (i, 0)),
        out_shape=jax.ShapeDtypeStruct((m, w_b), BF16),
        compiler_params=_cparams(("parallel",), 32),
        name="sgu_prompt",
    )(proj, proj, w, bcol, lg, lb)


CONV_PAD = 32
CONV_ROWS = 256


def _conv_tail(y, cb, lg, lb):
    return jax.nn.silu(_group_ln(y + cb, lg, lb))


def _conv_prompt_kernel(a_ref, gt_ref, w_ref, cb_ref, lg_ref, lb_ref, o_ref, tail_ref, pad_ref):
    seq = a_ref.shape[0]
    pad_ref[0:CONV_PAD, :] = jnp.zeros((CONV_PAD, LANES), F32)
    pad_ref[CONV_PAD:CONV_PAD + seq, :] = a_ref[...] * jax.nn.sigmoid(gt_ref[...])
    tail_ref[...] = pad_ref[CONV_PAD + seq - (CONV_W - 1):CONV_PAD + seq, :]
    w = w_ref[...]
    for c0 in range(0, seq, CONV_ROWS):
        acc = jnp.zeros((CONV_ROWS, LANES), F32)
        for j in range(CONV_W):
            start = CONV_PAD + c0 + j - (CONV_W - 1)
            acc = acc + pad_ref[start:start + CONV_ROWS, :] * w[j:j + 1, :]
        o_ref[c0:c0 + CONV_ROWS, :] = _conv_tail(acc, cb_ref[...], lg_ref[...], lb_ref[...]).astype(o_ref.dtype)


def conv_prompt(proj, w, cb, lg, lb, layer, batch, seq, w_c, cg_off):
    assert seq % CONV_ROWS == 0
    n_groups = w_c // LANES
    a_col = cg_off // LANES
    vec = lambda: pl.BlockSpec((None, 1, LANES), lambda b, c: (layer, 0, c))
    return pl.pallas_call(
        _conv_prompt_kernel,
        grid=(batch, n_groups),
        in_specs=[
            pl.BlockSpec((seq, LANES), lambda b, c: (b, a_col + c)),
            pl.BlockSpec((seq, LANES), lambda b, c: (b, a_col + n_groups + c)),
            pl.BlockSpec((None, CONV_W, LANES), lambda b, c: (layer, 0, c)),
            vec(), vec(), vec(),
        ],
        out_specs=[
            pl.BlockSpec((seq, LANES), lambda b, c: (b, c)),
            pl.BlockSpec((None, CONV_W - 1, LANES), lambda b, c: (b, 0, c)),
        ],
        out_shape=[
            jax.ShapeDtypeStruct((batch * seq, w_c), BF16),
            jax.ShapeDtypeStruct((batch, CONV_W - 1, w_c), F32),
        ],
        scratch_shapes=[pltpu.VMEM((CONV_PAD + seq, LANES), F32)],
        compiler_params=_cparams(("parallel", "parallel"), 32),
        name="conv_prompt",
    )(proj, proj, w, cb, lg, lb)


def _sample_mixer_kernel(bu_ref, bv_ref, a_ref, gt_ref, st_ref, sw_ref, bcol_ref, slg_ref, slb_ref,
                         cw_ref, cb_ref, clg_ref, clb_ref,
                         sgu_ref, conv_ref, cnew_ref, vn_ref, vpad_ref, cin_ref):
    dec_seq = bu_ref.shape[0]
    n_sgu = sw_ref.shape[0]
    n_conv = cw_ref.shape[1] // LANES
    state_rows = CONV_W - 1

    vpad_ref[...] = jnp.zeros_like(vpad_ref)
    for g in range(n_sgu):
        sl = slice(g * LANES, (g + 1) * LANES)
        vn = _group_ln(jax.nn.gelu(bv_ref[:, sl]), slg_ref[:, sl], slb_ref[:, sl])
        vn_ref[:, sl] = vn
        vpad_ref[0:dec_seq, sl] = vn
    for g in range(n_sgu):
        sl = slice(g * LANES, (g + 1) * LANES)
        s = _dot(_tril_weight(sw_ref[g]), vpad_ref[:, sl].astype(BF16)) + bcol_ref[g]
        sgu_ref[:, sl] = jax.nn.gelu(bu_ref[:, sl]) * s[0:dec_seq]

    cin_ref[0:state_rows, :] = st_ref[...]
    cin_ref[state_rows:state_rows + dec_seq, :] = a_ref[...] * jax.nn.sigmoid(gt_ref[...])
    cnew_ref[...] = cin_ref[dec_seq:dec_seq + state_rows, :]
    w = cw_ref[...]
    acc = jnp.zeros(a_ref.shape, F32)
    for j in range(CONV_W):
        acc = acc + cin_ref[j:j + dec_seq, :] * w[j:j + 1, :]
    for c in range(n_conv):
        sl = slice(c * LANES, (c + 1) * LANES)
        conv_ref[:, sl] = _conv_tail(acc[:, sl], cb_ref[:, sl], clg_ref[:, sl], clb_ref[:, sl])


def sample_mixer(proj, state_conv, sw, bcol, slg, slb, cw, cb, clg, clb, layer, dec_batch, dec_seq, w_b, w_c,
                 bu_off, cg_off):
    assert dec_seq <= SGU_CHUNK and dec_seq % SUBLANES == 0
    ms = dec_batch * dec_seq
    n_sgu = w_b // LANES
    bu_col = bu_off // w_b
    a_col = cg_off // w_c
    cin_rows = -(-(CONV_W - 1 + dec_seq) // SUBLANES) * SUBLANES
    lvec = lambda width: pl.BlockSpec((None, 1, width), lambda b: (layer, 0, 0))
    return pl.pallas_call(
        _sample_mixer_kernel,
        grid=(dec_batch,),
        in_specs=[
            pl.BlockSpec((dec_seq, w_b), lambda b: (b, bu_col)),
            pl.BlockSpec((dec_seq, w_b), lambda b: (b, bu_col + 1)),
            pl.BlockSpec((dec_seq, w_c), lambda b: (b, a_col)),
            pl.BlockSpec((dec_seq, w_c), lambda b: (b, a_col + 1)),
            pl.BlockSpec((None, None, CONV_W - 1, w_c), lambda b: (layer, b, 0, 0)),
            pl.BlockSpec((None, n_sgu, SGU_CHUNK, SGU_CHUNK), lambda b: (layer, 0, 0, 0)),
            pl.BlockSpec((None, n_sgu, SGU_CHUNK, LANES), lambda b: (layer, 0, 0, 0)),
            lvec(w_b), lvec(w_b),
            pl.BlockSpec((None, CONV_W, w_c), lambda b: (layer, 0, 0)),
            lvec(w_c), lvec(w_c), lvec(w_c),
        ],
        out_specs=[
            pl.BlockSpec((dec_seq, w_b), lambda b: (b, 0)),
            pl.BlockSpec((dec_seq, w_c), lambda b: (b, 0)),
            pl.BlockSpec((None, CONV_W - 1, w_c), lambda b: (b, 0, 0)),
            pl.BlockSpec((dec_seq, w_b), lambda b: (b, 0)),
        ],
        out_shape=[
            jax.ShapeDtypeStruct((ms, w_b), F32),
            jax.ShapeDtypeStruct((ms, w_c), F32),
            jax.ShapeDtypeStruct((dec_batch, CONV_W - 1, w_c), F32),
            jax.ShapeDtypeStruct((ms, w_b), F32),
        ],
        scratch_shapes=[pltpu.VMEM((SGU_CHUNK, w_b), F32), pltpu.VMEM((cin_rows, w_c), F32)],
        compiler_params=_cparams(("parallel",), 32),
        name="sample_mixer",
    )(proj, proj, proj, proj, state_conv, sw, bcol, slg, slb, cw, cb, clg, clb)


def _out_proj_kernel(a_ref, s_ref, c_ref, w_ref, x_ref, g_ref, o_ref, *w_copy):
    j = pl.program_id(1)
    tn = w_ref.shape[-1]
    w_a, w_b = a_ref.shape[1], s_ref.shape[1]
    w = _bf16_weight(w_ref, w_copy)
    acc = (_dot(a_ref[...].astype(BF16), w[0:w_a]) + _dot(s_ref[...].astype(BF16), w[w_a:w_a + w_b])
           + _dot(c_ref[...].astype(BF16), w[w_a + w_b:]))
    o_ref[:, pl.ds(pl.multiple_of(j * tn, tn), tn)] = acc

    @pl.when(j == pl.num_programs(1) - 1)
    def _():
        o_ref[...] = x_ref[...] + _rms(o_ref[...], g_ref[...])


def out_proj(attn, sgu, conv, w, x, g, layer, w_layer, tm, tn):
    m, d = x.shape
    w_a, w_b, w_c = attn.shape[1], sgu.shape[1], conv.shape[1]
    assert w_a + w_b + w_c == w.shape[1] and m % tm == 0 and d % tn == 0 and (w.dtype == BF16 or m == tm)
    copy_specs, copy_shapes = _weight_copy_out(w, (w.shape[1], tn), lambda i, j: (0, 0, j))
    out = pl.pallas_call(
        _out_proj_kernel,
        grid=(m // tm, d // tn),
        in_specs=[
            pl.BlockSpec((tm, w_a), lambda i, j: (i, 0)),
            pl.BlockSpec((tm, w_b), lambda i, j: (i, 0)),
            pl.BlockSpec((tm, w_c), lambda i, j: (i, 0)),
            pl.BlockSpec((None, w.shape[1], tn), lambda i, j: (w_layer, 0, j)),
            pl.BlockSpec((tm, d), lambda i, j: (i, 0)),
            pl.BlockSpec((None, 1, d), lambda i, j: (layer, 0, 0)),
        ],
        out_specs=[pl.BlockSpec((tm, d), lambda i, j: (i, 0))] + copy_specs,
        out_shape=[jax.ShapeDtypeStruct((m, d), F32)] + copy_shapes,
        compiler_params=_cparams(("parallel", "arbitrary"), 58),
        name=f"out_proj_m{m}",
    )(attn, sgu, conv, w, x, g)
    return out if copy_specs else out[0]


def _ffn_kernel(x_ref, gpre_ref, wup_ref, wdn_ref, gpost_ref, o_ref, *rest):
    *w_copies, xn_ref = rest
    f = pl.program_id(1)

    @pl.when(f == 0)
    def _():
        xn_ref[...] = _rms(x_ref[...], gpre_ref[...]).astype(BF16)
        o_ref[...] = jnp.zeros_like(o_ref)

    h = jnp.square(jnp.maximum(_dot(xn_ref[...], _bf16_weight(wup_ref, w_copies[:1])), 0.0)).astype(BF16)
    o_ref[...] += _dot(h, _bf16_weight(wdn_ref, w_copies[1:]))

    @pl.when(f == pl.num_programs(1) - 1)
    def _():
        o_ref[...] = x_ref[...] + _rms(o_ref[...], gpost_ref[...])


def ffn(x, gpre, wup, wdn, gpost, layer, w_layer, tm, tf):
    m, d = x.shape
    d_ff = wup.shape[-1]
    assert m % tm == 0 and d_ff % tf == 0 and wup.dtype == wdn.dtype and (wup.dtype == BF16 or m == tm)
    up_specs, up_shapes = _weight_copy_out(wup, (d, tf), lambda i, f: (0, 0, f))
    dn_specs, dn_shapes = _weight_copy_out(wdn, (tf, d), lambda i, f: (0, f, 0))
    out = pl.pallas_call(
        _ffn_kernel,
        grid=(m // tm, d_ff // tf),
        in_specs=[
            pl.BlockSpec((tm, d), lambda i, f: (i, 0), pipeline_mode=pl.Buffered(1)),
            pl.BlockSpec((None, 1, d), lambda i, f: (layer, 0, 0)),
            pl.BlockSpec((None, d, tf), lambda i, f: (w_layer, 0, f)),
            pl.BlockSpec((None, tf, d), lambda i, f: (w_layer, f, 0)),
            pl.BlockSpec((None, 1, d), lambda i, f: (layer, 0, 0)),
        ],
        out_specs=[pl.BlockSpec((tm, d), lambda i, f: (i, 0))] + up_specs + dn_specs,
        out_shape=[jax.ShapeDtypeStruct((m, d), F32)] + up_shapes + dn_shapes,
        scratch_shapes=[pltpu.VMEM((tm, d), BF16)],
        compiler_params=_cparams(("parallel", "arbitrary"), 58),
        name=f"ffn_m{m}",
    )(x, gpre, wup, wdn, gpost)
    return out if up_specs else out[0]


def _row_tile(m, target):
    t = min(m, target)
    while m % t:
        t //= 2
    return t


def kernel(x_prompt, x_sample, cache_k, cache_v, state_conv, page_table, g_mix_pre, w_in, sgu_norm_g, sgu_norm_b,
           sgu_w, sgu_b, conv_w, conv_b, conv_norm_g, conv_norm_b, w_out, g_mix_post, g_ffn_pre, w_up, w_down,
           g_ffn_post):
    batch, seq, d = x_prompt.shape
    dec_batch, dec_seq, _ = x_sample.shape
    depth = w_in.shape[0]
    w_a, w_b = d // 2, d // 4
    w_c = d - w_a - w_b
    n_kv = (w_a // HEAD_DIM) // KV_GROUP
    kv_w = n_kv * HEAD_DIM
    bu_off = w_a + 2 * kv_w
    cg_off = bu_off + 2 * w_b

    vec3 = lambda v: v.reshape(depth, 1, v.shape[-1])
    g_mix_pre3, g_mix_post3, g_ffn_pre3, g_ffn_post3 = map(vec3, (g_mix_pre, g_mix_post, g_ffn_pre, g_ffn_post))
    slg3, slb3, cb3, clg3, clb3 = map(vec3, (sgu_norm_g, sgu_norm_b, conv_b, conv_norm_g, conv_norm_b))
    bcol = jnp.broadcast_to(sgu_b[..., None], sgu_b.shape + (LANES,))

    mp, ms = batch * seq, dec_batch * dec_seq
    tm_p, tm_s = _row_tile(mp, 512), _row_tile(ms, 512)
    tn = _row_tile(d, 512)
    yp = x_prompt.reshape(mp, d)
    ys = x_sample.reshape(ms, d)
    cast_in_decode = ms == tm_s
    if not cast_in_decode:
        w_in, w_out, w_up, w_down = (w.astype(BF16) for w in (w_in, w_out, w_up, w_down))
    tf_s = _row_tile(w_up.shape[-1], 256) if cast_in_decode else tn

    def split(result, stacked, layer):
        return (result[0], result[1:], 0) if cast_in_decode else (result, stacked, layer)

    outs = [[] for _ in range(7)]
    for l in range(depth):
        sproj, (w_in_b,), wl = split(rms_inproj(ys, g_mix_pre3, w_in, l, l, tm_s, tn), (w_in,), l)
        sattn = moba_sample(sproj, cache_k, cache_v, page_table, l, dec_batch, dec_seq, w_a, kv_w)
        ssgu, sconv, sconv_new, svn = sample_mixer(sproj, state_conv, sgu_w, bcol, slg3, slb3, conv_w, cb3, clg3,
                                                   clb3, l, dec_batch, dec_seq, w_b, w_c, bu_off, cg_off)
        ys, (w_out_b,), _ = split(out_proj(sattn, ssgu, sconv, w_out, ys, g_mix_post3, l, l, tm_s, tn), (w_out,), l)
        ys, (w_up_b, w_dn_b), _ = split(ffn(ys, g_ffn_pre3, w_up, w_down, g_ffn_post3, l, l, tm_s, tf_s),
                                        (w_up, w_down), l)
        outs[3].append(sproj[:, w_a:w_a + kv_w].reshape(dec_batch, dec_seq, n_kv, HEAD_DIM))
        outs[4].append(sproj[:, w_a + kv_w:bu_off].reshape(dec_batch, dec_seq, n_kv, HEAD_DIM))
        outs[5].append(sconv_new)
        outs[6].append(svn.reshape(dec_batch, dec_seq, w_b))
        proj = rms_inproj(yp, g_mix_pre3, w_in_b, l, wl, tm_p, _row_tile(w_in.shape[-1], 1024))
        attn = moba_prompt(proj, batch, seq, w_a, kv_w)
        sgu = sgu_prompt(proj, sgu_w, bcol, slg3, slb3, l, _row_tile(mp, 512), w_b, bu_off)
        conv, conv_tail = conv_prompt(proj, conv_w, cb3, clg3, clb3, l, batch, seq, w_c, cg_off)
        yp = out_proj(attn, sgu, conv, w_out_b, yp, g_mix_post3, l, wl, tm_p, tn)
        yp = ffn(yp, g_ffn_pre3, w_up_b, w_dn_b, g_ffn_post3, l, wl, tm_p, tn)
        outs[0].append(proj[:, w_a:w_a + kv_w].reshape(batch, seq, n_kv, HEAD_DIM))
        outs[1].append(proj[:, w_a + kv_w:bu_off].reshape(batch, seq, n_kv, HEAD_DIM))
        outs[2].append(conv_tail)
    return (yp.reshape(batch, seq, d), ys.reshape(dec_batch, dec_seq, d)) + tuple(jnp.stack(o) for o in outs)
```

```python
import functools

import jax
import jax.numpy as jnp
from jax import lax
from jax.experimental import pallas as pl
from jax.experimental.pallas import tpu as pltpu

F32 = jnp.float32
BF16 = jnp.bfloat16

HEAD_DIM = 128
KV_GROUP = 4
MOBA_BLOCK = 256
MOBA_TOPK = 3
SGU_CHUNK = 128
CONV_W = 31
RMS_EPS = 1e-6
LN_EPS = 1e-5
MASKED = -1e30
EXP2_SCALE = HEAD_DIM ** -0.5 * 1.4426950408889634
LANES = 128
SUBLANES = 8
MIB = 1024 * 1024


def _cparams(sem, vmem_mib):
    return pltpu.CompilerParams(dimension_semantics=sem, vmem_limit_bytes=vmem_mib * MIB)


def _rms(x, g):
    ms = jnp.mean(x * x, axis=-1, keepdims=True)
    return x * lax.rsqrt(ms + RMS_EPS) * g


def _group_ln(x, g, b):
    mu = jnp.mean(x, axis=-1, keepdims=True)
    d = x - mu
    var = jnp.mean(d * d, axis=-1, keepdims=True)
    return d * lax.rsqrt(var + LN_EPS) * g + b


def _dot(a, b):
    return jnp.dot(a, b, preferred_element_type=F32)


def _dot_nt(a, b):
    return lax.dot_general(a, b, (((1,), (1,)), ((), ())), preferred_element_type=F32)


def _bf16_weight(w_ref, copy_refs):
    w = w_ref[...]
    if copy_refs:
        w = w.astype(BF16)
        copy_refs[0][...] = w
    return w


def _weight_copy_out(w, block, index_map):
    if w.dtype == BF16:
        return [], []
    return [pl.BlockSpec((None,) + block, index_map)], [jax.ShapeDtypeStruct((1,) + w.shape[1:], BF16)]


def _rms_inproj_kernel(x_ref, g_ref, w_ref, o_ref, *rest):
    *w_copy, h_ref = rest

    @pl.when(pl.program_id(1) == 0)
    def _():
        h_ref[...] = _rms(x_ref[...], g_ref[...]).astype(BF16)

    o_ref[...] = _dot(h_ref[...], _bf16_weight(w_ref, w_copy))


def rms_inproj(x, g, w, layer, w_layer, tm, tn):
    m, d = x.shape
    n = w.shape[-1]
    assert m % tm == 0 and n % tn == 0 and (w.dtype == BF16 or m == tm)
    copy_specs, copy_shapes = _weight_copy_out(w, (d, tn), lambda i, j: (0, 0, j))
    out = pl.pallas_call(
        _rms_inproj_kernel,
        grid=(m // tm, n // tn),
        in_specs=[
            pl.BlockSpec((tm, d), lambda i, j: (i, 0)),
            pl.BlockSpec((None, 1, d), lambda i, j: (layer, 0, 0)),
            pl.BlockSpec((None, d, tn), lambda i, j: (w_layer, 0, j)),
        ],
        out_specs=[pl.BlockSpec((tm, tn), lambda i, j: (i, j))] + copy_specs,
        out_shape=[jax.ShapeDtypeStruct((m, n), F32)] + copy_shapes,
        scratch_shapes=[pltpu.VMEM((tm, d), BF16)],
        compiler_params=_cparams(("parallel", "arbitrary"), 56),
        name=f"rms_inproj_m{m}",
    )(x, g, w)
    return out if copy_specs else out[0]


def _moba_prompt_kernel(q_ref, k_ref, v_ref, o_ref, kb_ref, vt_ref, s_ref, p_ref, *, nb):
    blk = MOBA_BLOCK
    rows = KV_GROUP * blk
    k = k_ref[...]
    kb_ref[...] = k.astype(BF16)
    vt_ref[...] = v_ref[...].T.astype(BF16)
    nb_pad = -(-nb // SUBLANES) * SUBLANES
    k_mean = jnp.concatenate(
        [jnp.mean(k[n * blk:(n + 1) * blk], axis=0, keepdims=True) for n in range(nb)]
        + [jnp.zeros((1, HEAD_DIM), F32)] * (nb_pad - nb), axis=0).astype(BF16)
    key_i = lax.broadcasted_iota(jnp.int32, (blk, rows), 0)
    row_i = lax.broadcasted_iota(jnp.int32, (blk, rows), 1) & (blk - 1)
    causal = key_i <= row_i

    for qi in range(nb):
        qs = q_ref[qi * blk:(qi + 1) * blk, :]
        qall = jnp.concatenate(
            [qs[:, g * HEAD_DIM:(g + 1) * HEAD_DIM] for g in range(KV_GROUP)], axis=0).astype(BF16)
        n_keys = (qi + 1) * blk
        s_ref[0:n_keys, :] = _dot_nt(kb_ref[0:n_keys, :], qall)

        def block_scores(j, qi=qi):
            s = s_ref[j * blk:(j + 1) * blk, :]
            return jnp.where(causal, s, MASKED) if j == qi else s

        bmax = [jnp.max(block_scores(j), axis=0, keepdims=True) for j in range(qi + 1)]
        if qi > MOBA_TOPK:
            gate = _dot_nt(k_mean, qall)
            n_iota = lax.broadcasted_iota(jnp.int32, gate.shape, 0)
            rank = jnp.zeros(gate.shape, jnp.int32)
            for j in range(qi):
                gj = gate[j:j + 1, :]
                rank = rank + jnp.where(gj > gate, 1, jnp.where((gj == gate) & (n_iota > j), 1, 0))
            picked = [rank[j:j + 1, :] < MOBA_TOPK for j in range(qi)]
            m = bmax[qi]
            for j in range(qi):
                m = jnp.maximum(m, jnp.where(picked[j], bmax[j], MASKED))
            off = [jnp.where(picked[j], m * EXP2_SCALE, -MASKED) for j in range(qi)] + [m * EXP2_SCALE]
        else:
            m = functools.reduce(jnp.maximum, bmax)
            off = [m * EXP2_SCALE] * (qi + 1)

        l = jnp.zeros((1, rows), F32)
        for j in range(qi + 1):
            p = jnp.exp2(block_scores(j) * EXP2_SCALE - off[j])
            l = l + jnp.sum(p, axis=0, keepdims=True)
            p_ref[j * blk:(j + 1) * blk, :] = p.astype(BF16)
        out = (_dot(vt_ref[:, 0:n_keys], p_ref[0:n_keys, :]) / l).T
        for g in range(KV_GROUP):
            o_ref[qi * blk:(qi + 1) * blk, g * HEAD_DIM:(g + 1) * HEAD_DIM] = (
                out[g * blk:(g + 1) * blk, :].astype(o_ref.dtype))


def moba_prompt(proj, batch, seq, w_a, kv_w):
    assert seq % MOBA_BLOCK == 0
    nq = seq // MOBA_BLOCK
    n_kv = kv_w // HEAD_DIM
    gw = KV_GROUP * HEAD_DIM
    rows = KV_GROUP * MOBA_BLOCK
    k_col = w_a // HEAD_DIM
    v_col = (w_a + kv_w) // HEAD_DIM
    return pl.pallas_call(
        functools.partial(_moba_prompt_kernel, nb=nq),
        grid=(batch, n_kv),
        in_specs=[
            pl.BlockSpec((seq, gw), lambda b, h: (b, h)),
            pl.BlockSpec((seq, HEAD_DIM), lambda b, h: (b, k_col + h)),
            pl.BlockSpec((seq, HEAD_DIM), lambda b, h: (b, v_col + h)),
        ],
        out_specs=pl.BlockSpec((seq, gw), lambda b, h: (b, h)),
        out_shape=jax.ShapeDtypeStruct((batch * seq, w_a), BF16),
        scratch_shapes=[
            pltpu.VMEM((seq, HEAD_DIM), BF16),
            pltpu.VMEM((HEAD_DIM, seq), BF16),
            pltpu.VMEM((seq, rows), F32),
            pltpu.VMEM((seq, rows), BF16),
        ],
        compiler_params=_cparams(("parallel", "parallel"), 56),
        name="moba_prompt",
    )(proj, proj, proj)


def _moba_sample_kernel(pt_ref, q_ref, kn_ref, vn_ref, ck_ref, cv_ref, o_ref,
                        qt_ref, s_ref, gate_ref, bmax_ref, sel_ref, m_ref, l_ref, acc_ref, buf_ref, sem,
                        *, layer, gp, n_grp, n_batch, n_past_blk, dec_seq, n_heads):
    ph = pl.program_id(1)
    g = pl.program_id(2)
    page, n_kv = buf_ref.shape[2], buf_ref.shape[3]
    chunk = gp * page
    blk_per_chunk = chunk // MOBA_BLOCK
    chunk_start = pl.multiple_of(g * chunk, chunk)
    steps_per_batch = 2 * n_grp
    n_steps = n_batch * steps_per_batch
    step = pl.program_id(0) * steps_per_batch + ph * n_grp + g
    slot = lax.rem(step, 2)

    def for_page_copies(of_step, act):
        b_of = lax.div(of_step, steps_per_batch)
        in_batch = lax.rem(of_step, steps_per_batch)
        ph_of, g_of = lax.div(in_batch, n_grp), lax.rem(in_batch, n_grp)
        to_slot = lax.rem(of_step, 2)
        for cache_ref, cache_pass in ((ck_ref, 0), (cv_ref, 1)):
            @pl.when(ph_of == cache_pass)
            def _(cache_ref=cache_ref):
                for i in range(gp):
                    page_id = pt_ref[b_of, g_of * gp + i]
                    act(pltpu.make_async_copy(cache_ref.at[layer, page_id], buf_ref.at[to_slot, i], sem.at[to_slot]))

    @pl.when(step == 0)
    def _():
        for_page_copies(step, lambda copy: copy.start())

    @pl.when(step + 1 < n_steps)
    def _():
        for_page_copies(step + 1, lambda copy: copy.start())

    for_page_copies(step, lambda copy: copy.wait())

    def head_rows():
        pages = [jnp.swapaxes(buf_ref[slot, i], 0, 1).astype(BF16) for i in range(gp)]
        return lambda c: jnp.concatenate([p[c] for p in pages], axis=0)

    def new_rows(ref, c):
        new = ref[:, c * HEAD_DIM:(c + 1) * HEAD_DIM]
        return jnp.concatenate([new, jnp.zeros((LANES - dec_seq, HEAD_DIM), F32)], axis=0).astype(BF16)

    def scores(key_rows):
        return sum(_dot(key_rows(c), qt_ref[c]) for c in range(n_kv))

    @pl.when((ph == 0) & (g == 0))
    def _():
        q = q_ref[...]
        zero = jnp.zeros((dec_seq, HEAD_DIM), F32)
        for c in range(n_kv):
            q_rows = jnp.concatenate(
                [q[:, h * HEAD_DIM:(h + 1) * HEAD_DIM] if h // KV_GROUP == c else zero for h in range(n_heads)], axis=0)
            qt_ref[c] = q_rows.T.astype(BF16)
        gate_ref[...] = jnp.zeros_like(gate_ref)
        bmax_ref[...] = jnp.zeros_like(bmax_ref)

    @pl.when(ph == 0)
    def _():
        s = scores(head_rows())
        s_ref[pl.ds(chunk_start, chunk), :] = s
        for n in range(blk_per_chunk):
            sb = s[n * MOBA_BLOCK:(n + 1) * MOBA_BLOCK]
            gate_ref[pl.ds(g * blk_per_chunk + n, 1), :] = jnp.mean(sb, axis=0, keepdims=True)
            bmax_ref[pl.ds(g * blk_per_chunk + n, 1), :] = jnp.max(sb, axis=0, keepdims=True)

    @pl.when((ph == 1) & (g == 0))
    def _():
        gate = gate_ref[...]
        n_iota = lax.broadcasted_iota(jnp.int32, gate.shape, 0)
        rank = jnp.zeros(gate.shape, jnp.int32)
        for j in range(n_past_blk):
            gj = gate[j:j + 1, :]
            rank = rank + jnp.where(gj > gate, 1, jnp.where((gj == gate) & (n_iota > j), 1, 0))
        sel = (rank < MOBA_TOPK) & (n_iota < n_past_blk)
        sel_ref[...] = jnp.where(sel, 1.0, 0.0)
        m = jnp.max(jnp.where(sel, bmax_ref[...], MASKED), axis=0, keepdims=True)
        so = scores(lambda c: new_rows(kn_ref, c))
        key_t = lax.broadcasted_iota(jnp.int32, so.shape, 0)
        q_t = lax.broadcasted_iota(jnp.int32, so.shape, 1) % dec_seq
        so = jnp.where(key_t <= q_t, so, MASKED)
        m = jnp.maximum(m, jnp.max(so, axis=0, keepdims=True))
        po = jnp.exp2((so - m) * EXP2_SCALE).T
        m_ref[...] = m
        l_ref[...] = jnp.sum(po, axis=-1, keepdims=True)
        for c in range(n_kv):
            acc_ref[:, c * HEAD_DIM:(c + 1) * HEAD_DIM] = _dot(po.astype(BF16), new_rows(vn_ref, c))

    @pl.when(ph == 1)
    def _():
        p = jnp.exp2((s_ref[pl.ds(chunk_start, chunk), :] - m_ref[...]) * EXP2_SCALE)
        p = jnp.concatenate(
            [jnp.where(sel_ref[pl.ds(g * blk_per_chunk + n, 1), :] > 0.5, p[n * MOBA_BLOCK:(n + 1) * MOBA_BLOCK], 0.0)
             for n in range(blk_per_chunk)], axis=0).T
        l_ref[...] += jnp.sum(p, axis=-1, keepdims=True)
        pb = p.astype(BF16)
        v_rows = head_rows()
        for c in range(n_kv):
            acc_ref[:, c * HEAD_DIM:(c + 1) * HEAD_DIM] += _dot(pb, v_rows(c))

    @pl.when((ph == 1) & (g == n_grp - 1))
    def _():
        out = acc_ref[...] / l_ref[...]
        for h in range(n_heads):
            c = h // KV_GROUP
            o_ref[:, h * HEAD_DIM:(h + 1) * HEAD_DIM] = out[h * dec_seq:(h + 1) * dec_seq,
                                                            c * HEAD_DIM:(c + 1) * HEAD_DIM]


def moba_sample(proj, cache_k, cache_v, page_table, layer, dec_batch, dec_seq, w_a, kv_w):
    page, n_kv = cache_k.shape[2], cache_k.shape[3]
    n_pages = page_table.shape[1]
    past = n_pages * page
    assert past % MOBA_BLOCK == 0 and dec_seq <= MOBA_BLOCK and dec_seq % SUBLANES == 0 and dec_seq <= LANES
    n_past_blk = past // MOBA_BLOCK
    n_heads = w_a // HEAD_DIM
    hq = n_heads * dec_seq
    gp = 16
    while n_pages % gp or (gp * page) % MOBA_BLOCK:
        gp //= 2
    n_grp = n_pages // gp
    blk_rows = -(-n_past_blk // SUBLANES) * SUBLANES
    k_col = w_a // kv_w
    grid_spec = pltpu.PrefetchScalarGridSpec(
        num_scalar_prefetch=1,
        grid=(dec_batch, 2, n_grp),
        in_specs=[
            pl.BlockSpec((dec_seq, w_a), lambda b, ph, g, pt: (b, 0)),
            pl.BlockSpec((dec_seq, kv_w), lambda b, ph, g, pt: (b, k_col)),
            pl.BlockSpec((dec_seq, kv_w), lambda b, ph, g, pt: (b, k_col + 1)),
            pl.BlockSpec(memory_space=pl.ANY),
            pl.BlockSpec(memory_space=pl.ANY),
        ],
        out_specs=pl.BlockSpec((dec_seq, w_a), lambda b, ph, g, pt: (b, 0)),
        scratch_shapes=[
            pltpu.VMEM((n_kv, HEAD_DIM, hq), BF16),
            pltpu.VMEM((past, hq), F32),
            pltpu.VMEM((blk_rows, hq), F32),
            pltpu.VMEM((blk_rows, hq), F32),
            pltpu.VMEM((blk_rows, hq), F32),
            pltpu.VMEM((1, hq), F32),
            pltpu.VMEM((hq, 1), F32),
            pltpu.VMEM((hq, kv_w), F32),
            pltpu.VMEM((2, gp, page, n_kv, HEAD_DIM), F32),
            pltpu.SemaphoreType.DMA((2,)),
        ],
    )
    kernel = functools.partial(_moba_sample_kernel, layer=layer, gp=gp, n_grp=n_grp, n_batch=dec_batch,
                               n_past_blk=n_past_blk, dec_seq=dec_seq, n_heads=n_heads)
    return pl.pallas_call(
        kernel,
        grid_spec=grid_spec,
        out_shape=jax.ShapeDtypeStruct((dec_batch * dec_seq, w_a), F32),
        compiler_params=_cparams(("arbitrary", "arbitrary", "arbitrary"), 48),
        name="moba_sample",
    )(page_table, proj, proj, proj, cache_k, cache_v)


def _tril_weight(w):
    t_i = lax.broadcasted_iota(jnp.int32, w.shape, 0)
    s_i = lax.broadcasted_iota(jnp.int32, w.shape, 1)
    return jnp.where(t_i >= s_i, w, 0.0).astype(BF16)


def _sgu_kernel(bu_ref, bv_ref, w_ref, bcol_ref, lg_ref, lb_ref, o_ref):
    rows = bu_ref.shape[0]
    n_groups = w_ref.shape[0]
    for g in range(n_groups):
        sl = slice(g * LANES, (g + 1) * LANES)
        vn = _group_ln(jax.nn.gelu(bv_ref[:, sl]), lg_ref[:, sl], lb_ref[:, sl]).astype(BF16)
        u = jax.nn.gelu(bu_ref[:, sl])
        wg = _tril_weight(w_ref[g])
        for c in range(rows // SGU_CHUNK):
            rs = slice(c * SGU_CHUNK, (c + 1) * SGU_CHUNK)
            s = _dot(wg, vn[rs]) + bcol_ref[g]
            o_ref[rs, sl] = (u[rs] * s).astype(o_ref.dtype)


def sgu_prompt(proj, w, bcol, lg, lb, layer, rows, w_b, bu_off):
    m = proj.shape[0]
    n_groups = w_b // LANES
    bu_col = bu_off // w_b
    return pl.pallas_call(
        _sgu_kernel,
        grid=(m // rows,),
        in_specs=[
            pl.BlockSpec((rows, w_b), lambda i: (i, bu_col)),
            pl.BlockSpec((rows, w_b), lambda i: (i, bu_col + 1)),
            pl.BlockSpec((None, n_groups, SGU_CHUNK, SGU_CHUNK), lambda i: (layer, 0, 0, 0)),
            pl.BlockSpec((None, n_groups, SGU_CHUNK, LANES), lambda i: (layer, 0, 0, 0)),
            pl.BlockSpec((None, 1, w_b), lambda i: (layer, 0, 0)),
            pl.BlockSpec((None, 1, w_b), lambda i: (layer, 0, 0)),
        ],
        out_specs=pl.BlockSpec((rows, w_b), lambda i: (i, 0)),
        out_shape=jax.ShapeDtypeStruct((m, w_b), BF16),
        compiler_params=_cparams(("parallel",), 32),
        name="sgu_prompt",
    )(proj, proj, w, bcol, lg, lb)


CONV_PAD = 32
CONV_ROWS = 256


def _conv_tail(y, cb, lg, lb):
    return jax.nn.silu(_group_ln(y + cb, lg, lb))


def _conv_prompt_kernel(a_ref, gt_ref, w_ref, cb_ref, lg_ref, lb_ref, o_ref, tail_ref, pad_ref):
    seq = a_ref.shape[0]
    pad_ref[0:CONV_PAD, :] = jnp.zeros((CONV_PAD, LANES), F32)
    pad_ref[CONV_PAD:CONV_PAD + seq, :] = a_ref[...] * jax.nn.sigmoid(gt_ref[...])
    tail_ref[...] = pad_ref[CONV_PAD + seq - (CONV_W - 1):CONV_PAD + seq, :]
    w = w_ref[...]
    for c0 in range(0, seq, CONV_ROWS):
        acc = jnp.zeros((CONV_ROWS, LANES), F32)
        for j in range(CONV_W):
            start = CONV_PAD + c0 + j - (CONV_W - 1)
            acc = acc + pad_ref[start:start + CONV_ROWS, :] * w[j:j + 1, :]
        o_ref[c0:c0 + CONV_ROWS, :] = _conv_tail(acc, cb_ref[...], lg_ref[...], lb_ref[...]).astype(o_ref.dtype)


def conv_prompt(proj, w, cb, lg, lb, layer, batch, seq, w_c, cg_off):
    assert seq % CONV_ROWS == 0
    n_groups = w_c // LANES
    a_col = cg_off // LANES
    vec = lambda: pl.BlockSpec((None, 1, LANES), lambda b, c: (layer, 0, c))
    return pl.pallas_call(
        _conv_prompt_kernel,
        grid=(batch, n_groups),
        in_specs=[
            pl.BlockSpec((seq, LANES), lambda b, c: (b, a_col + c)),
            pl.BlockSpec((seq, LANES), lambda b, c: (b, a_col + n_groups + c)),
            pl.BlockSpec((None, CONV_W, LANES), lambda b, c: (layer, 0, c)),
            vec(), vec(), vec(),
        ],
        out_specs=[
            pl.BlockSpec((seq, LANES), lambda b, c: (b, c)),
            pl.BlockSpec((None, CONV_W - 1, LANES), lambda b, c: (b, 0, c)),
        ],
        out_shape=[
            jax.ShapeDtypeStruct((batch * seq, w_c), BF16),
            jax.ShapeDtypeStruct((batch, CONV_W - 1, w_c), F32),
        ],
        scratch_shapes=[pltpu.VMEM((CONV_PAD + seq, LANES), F32)],
        compiler_params=_cparams(("parallel", "parallel"), 32),
        name="conv_prompt",
    )(proj, proj, w, cb, lg, lb)


def _sample_mixer_kernel(bu_ref, bv_ref, a_ref, gt_ref, st_ref, sw_ref, bcol_ref, slg_ref, slb_ref,
                         cw_ref, cb_ref, clg_ref, clb_ref,
                         sgu_ref, conv_ref, cnew_ref, vn_ref, vpad_ref, cin_ref):
    dec_seq = bu_ref.shape[0]
    n_sgu = sw_ref.shape[0]
    n_conv = cw_ref.shape[1] // LANES
    state_rows = CONV_W - 1

    vpad_ref[...] = jnp.zeros_like(vpad_ref)
    for g in range(n_sgu):
        sl = slice(g * LANES, (g + 1) * LANES)
        vn = _group_ln(jax.nn.gelu(bv_ref[:, sl]), slg_ref[:, sl], slb_ref[:, sl])
        vn_ref[:, sl] = vn
        vpad_ref[0:dec_seq, sl] = vn
    for g in range(n_sgu):
        sl = slice(g * LANES, (g + 1) * LANES)
        s = _dot(_tril_weight(sw_ref[g]), vpad_ref[:, sl].astype(BF16)) + bcol_ref[g]
        sgu_ref[:, sl] = jax.nn.gelu(bu_ref[:, sl]) * s[0:dec_seq]

    cin_ref[0:state_rows, :] = st_ref[...]
    cin_ref[state_rows:state_rows + dec_seq, :] = a_ref[...] * jax.nn.sigmoid(gt_ref[...])
    cnew_ref[...] = cin_ref[dec_seq:dec_seq + state_rows, :]
    w = cw_ref[...]
    acc = jnp.zeros(a_ref.shape, F32)
    for j in range(CONV_W):
        acc = acc + cin_ref[j:j + dec_seq, :] * w[j:j + 1, :]
    for c in range(n_conv):
        sl = slice(c * LANES, (c + 1) * LANES)
        conv_ref[:, sl] = _conv_tail(acc[:, sl], cb_ref[:, sl], clg_ref[:, sl], clb_ref[:, sl])


def sample_mixer(proj, state_conv, sw, bcol, slg, slb, cw, cb, clg, clb, layer, dec_batch, dec_seq, w_b, w_c,
                 bu_off, cg_off):
    assert dec_seq <= SGU_CHUNK and dec_seq % SUBLANES == 0
    ms = dec_batch * dec_seq
    n_sgu = w_b // LANES
    bu_col = bu_off // w_b
    a_col = cg_off // w_c
    cin_rows = -(-(CONV_W - 1 + dec_seq) // SUBLANES) * SUBLANES
    lvec = lambda width: pl.BlockSpec((None, 1, width), lambda b: (layer, 0, 0))
    return pl.pallas_call(
        _sample_mixer_kernel,
        grid=(dec_batch,),
        in_specs=[
            pl.BlockSpec((dec_seq, w_b), lambda b: (b, bu_col)),
            pl.BlockSpec((dec_seq, w_b), lambda b: (b, bu_col + 1)),
            pl.BlockSpec((dec_seq, w_c), lambda b: (b, a_col)),
            pl.BlockSpec((dec_seq, w_c), lambda b: (b, a_col + 1)),
            pl.BlockSpec((None, None, CONV_W - 1, w_c), lambda b: (layer, b, 0, 0)),
            pl.BlockSpec((None, n_sgu, SGU_CHUNK, SGU_CHUNK), lambda b: (layer, 0, 0, 0)),
            pl.BlockSpec((None, n_sgu, SGU_CHUNK, LANES), lambda b: (layer, 0, 0, 0)),
            lvec(w_b), lvec(w_b),
            pl.BlockSpec((None, CONV_W, w_c), lambda b: (layer, 0, 0)),
            lvec(w_c), lvec(w_c), lvec(w_c),
        ],
        out_specs=[
            pl.BlockSpec((dec_seq, w_b), lambda b: (b, 0)),
            pl.BlockSpec((dec_seq, w_c), lambda b: (b, 0)),
            pl.BlockSpec((None, CONV_W - 1, w_c), lambda b: (b, 0, 0)),
            pl.BlockSpec((dec_seq, w_b), lambda b: (b, 0)),
        ],
        out_shape=[
            jax.ShapeDtypeStruct((ms, w_b), F32),
            jax.ShapeDtypeStruct((ms, w_c), F32),
            jax.ShapeDtypeStruct((dec_batch, CONV_W - 1, w_c), F32),
            jax.ShapeDtypeStruct((ms, w_b), F32),
        ],
        scratch_shapes=[pltpu.VMEM((SGU_CHUNK, w_b), F32), pltpu.VMEM((cin_rows, w_c), F32)],
        compiler_params=_cparams(("parallel",), 32),
        name="sample_mixer",
    )(proj, proj, proj, proj, state_conv, sw, bcol, slg, slb, cw, cb, clg, clb)


def _out_proj_kernel(a_ref, s_ref, c_ref, w_ref, x_ref, g_ref, o_ref, *w_copy):
    j = pl.program_id(1)
    tn = w_ref.shape[-1]
    w_a, w_b = a_ref.shape[1], s_ref.shape[1]
    w = _bf16_weight(w_ref, w_copy)
    acc = (_dot(a_ref[...].astype(BF16), w[0:w_a]) + _dot(s_ref[...].astype(BF16), w[w_a:w_a + w_b])
           + _dot(c_ref[...].astype(BF16), w[w_a + w_b:]))
    o_ref[:, pl.ds(pl.multiple_of(j * tn, tn), tn)] = acc

    @pl.when(j == pl.num_programs(1) - 1)
    def _():
        o_ref[...] = x_ref[...] + _rms(o_ref[...], g_ref[...])


def out_proj(attn, sgu, conv, w, x, g, layer, w_layer, tm, tn):
    m, d = x.shape
    w_a, w_b, w_c = attn.shape[1], sgu.shape[1], conv.shape[1]
    assert w_a + w_b + w_c == w.shape[1] and m % tm == 0 and d % tn == 0 and (w.dtype == BF16 or m == tm)
    copy_specs, copy_shapes = _weight_copy_out(w, (w.shape[1], tn), lambda i, j: (0, 0, j))
    out = pl.pallas_call(
        _out_proj_kernel,
        grid=(m // tm, d // tn),
        in_specs=[
            pl.BlockSpec((tm, w_a), lambda i, j: (i, 0)),
            pl.BlockSpec((tm, w_b), lambda i, j: (i, 0)),
            pl.BlockSpec((tm, w_c), lambda i, j: (i, 0)),
            pl.BlockSpec((None, w.shape[1], tn), lambda i, j: (w_layer, 0, j)),
            pl.BlockSpec((tm, d), lambda i, j: (i, 0)),
            pl.BlockSpec((None, 1, d), lambda i, j: (layer, 0, 0)),
        ],
        out_specs=[pl.BlockSpec((tm, d), lambda i, j: (i, 0))] + copy_specs,
        out_shape=[jax.ShapeDtypeStruct((m, d), F32)] + copy_shapes,
        compiler_params=_cparams(("parallel", "arbitrary"), 58),
        name=f"out_proj_m{m}",
    )(attn, sgu, conv, w, x, g)
    return out if copy_specs else out[0]


def _ffn_kernel(x_ref, gpre_ref, wup_ref, wdn_ref, gpost_ref, o_ref, *rest):
    *w_copies, xn_ref = rest
    f = pl.program_id(1)

    @pl.when(f == 0)
    def _():
        xn_ref[...] = _rms(x_ref[...], gpre_ref[...]).astype(BF16)
        o_ref[...] = jnp.zeros_like(o_ref)

    h = jnp.square(jnp.maximum(_dot(xn_ref[...], _bf16_weight(wup_ref, w_copies[:1])), 0.0)).astype(BF16)
    o_ref[...] += _dot(h, _bf16_weight(wdn_ref, w_copies[1:]))

    @pl.when(f == pl.num_programs(1) - 1)
    def _():
        o_ref[...] = x_ref[...] + _rms(o_ref[...], gpost_ref[...])


def ffn(x, gpre, wup, wdn, gpost, layer, w_layer, tm, tf):
    m, d = x.shape
    d_ff = wup.shape[-1]
    assert m % tm == 0 and d_ff % tf == 0 and wup.dtype == wdn.dtype and (wup.dtype == BF16 or m == tm)
    up_specs, up_shapes = _weight_copy_out(wup, (d, tf), lambda i, f: (0, 0, f))
    dn_specs, dn_shapes = _weight_copy_out(wdn, (tf, d), lambda i, f: (0, f, 0))
    out = pl.pallas_call(
        _ffn_kernel,
        grid=(m // tm, d_ff // tf),
        in_specs=[
            pl.BlockSpec((tm, d), lambda i, f: (i, 0), pipeline_mode=pl.Buffered(1)),
            pl.BlockSpec((None, 1, d), lambda i, f: (layer, 0, 0)),
            pl.BlockSpec((None, d, tf), lambda i, f: (w_layer, 0, f)),
            pl.BlockSpec((None, tf, d), lambda i, f: (w_layer, f, 0)),
            pl.BlockSpec((None, 1, d), lambda i, f: (layer, 0, 0)),
        ],
        out_specs=[pl.BlockSpec((tm, d), lambda i, f: (i, 0))] + up_specs + dn_specs,
        out_shape=[jax.ShapeDtypeStruct((m, d), F32)] + up_shapes + dn_shapes,
        scratch_shapes=[pltpu.VMEM((tm, d), BF16)],
        compiler_params=_cparams(("parallel", "arbitrary"), 58),
        name=f"ffn_m{m}",
    )(x, gpre, wup, wdn, gpost)
    return out if up_specs else out[0]


def _row_tile(m, target):
    t = min(m, target)
    while m % t:
        t //= 2
    return t


def kernel(x_prompt, x_sample, cache_k, cache_v, state_conv, page_table, g_mix_pre, w_in, sgu_norm_g, sgu_norm_b,
           sgu_w, sgu_b, conv_w, conv_b, conv_norm_g, conv_norm_b, w_out, g_mix_post, g_ffn_pre, w_up, w_down,
           g_ffn_post):
    batch, seq, d = x_prompt.shape
    dec_batch, dec_seq, _ = x_sample.shape
    depth = w_in.shape[0]
    w_a, w_b = d // 2, d // 4
    w_c = d - w_a - w_b
    n_kv = (w_a // HEAD_DIM) // KV_GROUP
    kv_w = n_kv * HEAD_DIM
    bu_off = w_a + 2 * kv_w
    cg_off = bu_off + 2 * w_b

    vec3 = lambda v: v.reshape(depth, 1, v.shape[-1])
    g_mix_pre3, g_mix_post3, g_ffn_pre3, g_ffn_post3 = map(vec3, (g_mix_pre, g_mix_post, g_ffn_pre, g_ffn_post))
    slg3, slb3, cb3, clg3, clb3 = map(vec3, (sgu_norm_g, sgu_norm_b, conv_b, conv_norm_g, conv_norm_b))
    bcol = jnp.broadcast_to(sgu_b[..., None], sgu_b.shape + (LANES,))

    mp, ms = batch * seq, dec_batch * dec_seq
    tm_p, tm_s = _row_tile(mp, 512), _row_tile(ms, 512)
    tn = _row_tile(d, 512)
    yp = x_prompt.reshape(mp, d)
    ys = x_sample.reshape(ms, d)
    cast_in_decode = ms == tm_s
    if not cast_in_decode:
        w_in, w_out, w_up, w_down = (w.astype(BF16) for w in (w_in, w_out, w_up, w_down))
    tf_s = _row_tile(w_up.shape[-1], 256) if cast_in_decode else tn

    def split(result, stacked, layer):
        return (result[0], result[1:], 0) if cast_in_decode else (result, stacked, layer)

    outs = [[] for _ in range(7)]
    for l in range(depth):
        sproj, (w_in_b,), wl = split(rms_inproj(ys, g_mix_pre3, w_in, l, l, tm_s, tn), (w_in,), l)
        sattn = moba_sample(sproj, cache_k, cache_v, page_table, l, dec_batch, dec_seq, w_a, kv_w)
        ssgu, sconv, sconv_new, svn = sample_mixer(sproj, state_conv, sgu_w, bcol, slg3, slb3, conv_w, cb3, clg3,
                                                   clb3, l, dec_batch, dec_seq, w_b, w_c, bu_off, cg_off)
        ys, (w_out_b,), _ = split(out_proj(sattn, ssgu, sconv, w_out, ys, g_mix_post3, l, l, tm_s, tn), (w_out,), l)
        ys, (w_up_b, w_dn_b), _ = split(ffn(ys, g_ffn_pre3, w_up, w_down, g_ffn_post3, l, l, tm_s, tf_s),
                                        (w_up, w_down), l)
        outs[3].append(sproj[:, w_a:w_a + kv_w].reshape(dec_batch, dec_seq, n_kv, HEAD_DIM))
        outs[4].append(sproj[:, w_a + kv_w:bu_off].reshape(dec_batch, dec_seq, n_kv, HEAD_DIM))
        outs[5].append(sconv_new)
        outs[6].append(svn.reshape(dec_batch, dec_seq, w_b))
        proj = rms_inproj(yp, g_mix_pre3, w_in_b, l, wl, tm_p, _row_tile(w_in.shape[-1], 1024))
        attn = moba_prompt(proj, batch, seq, w_a, kv_w)
        sgu = sgu_prompt(proj, sgu_w, bcol, slg3, slb3, l, _row_tile(mp, 512), w_b, bu_off)
        conv, conv_tail = conv_prompt(proj, conv_w, cb3, clg3, clb3, l, batch, seq, w_c, cg_off)
        yp = out_proj(attn, sgu, conv, w_out_b, yp, g_mix_post3, l, wl, tm_p, tn)
        yp = ffn(yp, g_ffn_pre3, w_up_b, w_dn_b, g_ffn_post3, l, wl, tm_p, tn)
        outs[0].append(proj[:, w_a:w_a + kv_w].reshape(batch, seq, n_kv, HEAD_DIM))
        outs[1].append(proj[:, w_a + kv_w:bu_off].reshape(batch, seq, n_kv, HEAD_DIM))
        outs[2].append(conv_tail)
    return (yp.reshape(batch, seq, d), ys.reshape(dec_batch, dec_seq, d)) + tuple(jnp.stack(o) for o in outs)
```

```python
import functools

import jax
import jax.numpy as jnp
from jax import lax
from jax.experimental import pallas as pl
from jax.experimental.pallas import tpu as pltpu

F32 = jnp.float32
BF16 = jnp.bfloat16

HEAD_DIM = 128
KV_GROUP = 4
MOBA_BLOCK = 256
MOBA_TOPK = 3
SGU_CHUNK = 128
CONV_W = 31
RMS_EPS = 1e-6
LN_EPS = 1e-5
MASKED = -1e30
EXP2_SCALE = HEAD_DIM ** -0.5 * 1.4426950408889634
LANES = 128
SUBLANES = 8
MIB = 1024 * 1024


def _cparams(sem, vmem_mib):
    return pltpu.CompilerParams(dimension_semantics=sem, vmem_limit_bytes=vmem_mib * MIB)


def _rms(x, g):
    ms = jnp.mean(x * x, axis=-1, keepdims=True)
    return x * lax.rsqrt(ms + RMS_EPS) * g


def _group_ln(x, g, b):
    mu = jnp.mean(x, axis=-1, keepdims=True)
    d = x - mu
    var = jnp.mean(d * d, axis=-1, keepdims=True)
    return d * lax.rsqrt(var + LN_EPS) * g + b


def _dot(a, b):
    return jnp.dot(a, b, preferred_element_type=F32)


def _dot_nt(a, b):
    return lax.dot_general(a, b, (((1,), (1,)), ((), ())), preferred_element_type=F32)


def _bf16_weight(w_ref, copy_refs):
    w = w_ref[...]
    if copy_refs:
        w = w.astype(BF16)
        copy_refs[0][...] = w
    return w


def _weight_copy_out(w, block, index_map):
    if w.dtype == BF16:
        return [], []
    return [pl.BlockSpec((None,) + block, index_map)], [jax.ShapeDtypeStruct((1,) + w.shape[1:], BF16)]


def _rms_inproj_kernel(x_ref, g_ref, w_ref, o_ref, *rest):
    *w_copy, h_ref = rest

    @pl.when(pl.program_id(1) == 0)
    def _():
        h_ref[...] = _rms(x_ref[...], g_ref[...]).astype(BF16)

    o_ref[...] = _dot(h_ref[...], _bf16_weight(w_ref, w_copy))


def rms_inproj(x, g, w, layer, w_layer, tm, tn):
    m, d = x.shape
    n = w.shape[-1]
    assert m % tm == 0 and n % tn == 0 and (w.dtype == BF16 or m == tm)
    copy_specs, copy_shapes = _weight_copy_out(w, (d, tn), lambda i, j: (0, 0, j))
    out = pl.pallas_call(
        _rms_inproj_kernel,
        grid=(m // tm, n // tn),
        in_specs=[
            pl.BlockSpec((tm, d), lambda i, j: (i, 0)),
            pl.BlockSpec((None, 1, d), lambda i, j: (layer, 0, 0)),
            pl.BlockSpec((None, d, tn), lambda i, j: (w_layer, 0, j)),
        ],
        out_specs=[pl.BlockSpec((tm, tn), lambda i, j: (i, j))] + copy_specs,
        out_shape=[jax.ShapeDtypeStruct((m, n), F32)] + copy_shapes,
        scratch_shapes=[pltpu.VMEM((tm, d), BF16)],
        compiler_params=_cparams(("parallel", "arbitrary"), 56),
        name=f"rms_inproj_m{m}",
    )(x, g, w)
    return out if copy_specs else out[0]


def _moba_prompt_kernel(q_ref, k_ref, v_ref, o_ref, kb_ref, vt_ref, s_ref, p_ref, *, nb):
    blk = MOBA_BLOCK
    rows = KV_GROUP * blk
    k = k_ref[...]
    kb_ref[...] = k.astype(BF16)
    vt_ref[...] = v_ref[...].T.astype(BF16)
    nb_pad = -(-nb // SUBLANES) * SUBLANES
    k_mean = jnp.concatenate(
        [jnp.mean(k[n * blk:(n + 1) * blk], axis=0, keepdims=True) for n in range(nb)]
        + [jnp.zeros((1, HEAD_DIM), F32)] * (nb_pad - nb), axis=0).astype(BF16)
    key_i = lax.broadcasted_iota(jnp.int32, (blk, rows), 0)
    row_i = lax.broadcasted_iota(jnp.int32, (blk, rows), 1) & (blk - 1)
    causal = key_i <= row_i

    for qi in range(nb):
        qs = q_ref[qi * blk:(qi + 1) * blk, :]
        qall = jnp.concatenate(
            [qs[:, g * HEAD_DIM:(g + 1) * HEAD_DIM] for g in range(KV_GROUP)], axis=0).astype(BF16)
        n_keys = (qi + 1) * blk
        s_ref[0:n_keys, :] = _dot_nt(kb_ref[0:n_keys, :], qall)

        def block_scores(j, qi=qi):
            s = s_ref[j * blk:(j + 1) * blk, :]
            return jnp.where(causal, s, MASKED) if j == qi else s

        bmax = [jnp.max(block_scores(j), axis=0, keepdims=True) for j in range(qi + 1)]
        if qi > MOBA_TOPK:
            gate = _dot_nt(k_mean, qall)
            n_iota = lax.broadcasted_iota(jnp.int32, gate.shape, 0)
            rank = jnp.zeros(gate.shape, jnp.int32)
            for j in range(qi):
                gj = gate[j:j + 1, :]
                rank = rank + jnp.where(gj > gate, 1, jnp.where((gj == gate) & (n_iota > j), 1, 0))
            picked = [rank[j:j + 1, :] < MOBA_TOPK for j in range(qi)]
            m = bmax[qi]
            for j in range(qi):
                m = jnp.maximum(m, jnp.where(picked[j], bmax[j], MASKED))
            off = [jnp.where(picked[j], m * EXP2_SCALE, -MASKED) for j in range(qi)] + [m * EXP2_SCALE]
        else:
            m = functools.reduce(jnp.maximum, bmax)
            off = [m * EXP2_SCALE] * (qi + 1)

        l = jnp.zeros((1, rows), F32)
        for j in range(qi + 1):
            p = jnp.exp2(block_scores(j) * EXP2_SCALE - off[j])
            l = l + jnp.sum(p, axis=0, keepdims=True)
            p_ref[j * blk:(j + 1) * blk, :] = p.astype(BF16)
        out = (_dot(vt_ref[:, 0:n_keys], p_ref[0:n_keys, :]) / l).T
        for g in range(KV_GROUP):
            o_ref[qi * blk:(qi + 1) * blk, g * HEAD_DIM:(g + 1) * HEAD_DIM] = (
                out[g * blk:(g + 1) * blk, :].astype(o_ref.dtype))


def moba_prompt(proj, batch, seq, w_a, kv_w):
    assert seq % MOBA_BLOCK == 0
    nq = seq // MOBA_BLOCK
    n_kv = kv_w // HEAD_DIM
    gw = KV_GROUP * HEAD_DIM
    rows = KV_GROUP * MOBA_BLOCK
    k_col = w_a // HEAD_DIM
    v_col = (w_a + kv_w) // HEAD_DIM
    return pl.pallas_call(
        functools.partial(_moba_prompt_kernel, nb=nq),
        grid=(batch, n_kv),
        in_specs=[
            pl.BlockSpec((seq, gw), lambda b, h: (b, h)),
            pl.BlockSpec((seq, HEAD_DIM), lambda b, h: (b, k_col + h)),
            pl.BlockSpec((seq, HEAD_DIM), lambda b, h: (b, v_col + h)),
        ],
        out_specs=pl.BlockSpec((seq, gw), lambda b, h: (b, h)),
        out_shape=jax.ShapeDtypeStruct((batch * seq, w_a), BF16),
        scratch_shapes=[
            pltpu.VMEM((seq, HEAD_DIM), BF16),
            pltpu.VMEM((HEAD_DIM, seq), BF16),
            pltpu.VMEM((seq, rows), F32),
            pltpu.VMEM((seq, rows), BF16),
        ],
        compiler_params=_cparams(("parallel", "parallel"), 56),
        name="moba_prompt",
    )(proj, proj, proj)


def _moba_sample_kernel(pt_ref, q_ref, kn_ref, vn_ref, ck_ref, cv_ref, o_ref,
                        qt_ref, s_ref, gate_ref, bmax_ref, sel_ref, m_ref, l_ref, acc_ref, buf_ref, sem,
                        *, layer, gp, n_grp, n_batch, n_past_blk, dec_seq, n_heads):
    ph = pl.program_id(1)
    g = pl.program_id(2)
    page, n_kv = buf_ref.shape[2], buf_ref.shape[3]
    chunk = gp * page
    blk_per_chunk = chunk // MOBA_BLOCK
    chunk_start = pl.multiple_of(g * chunk, chunk)
    steps_per_batch = 2 * n_grp
    n_steps = n_batch * steps_per_batch
    step = pl.program_id(0) * steps_per_batch + ph * n_grp + g
    slot = lax.rem(step, 2)

    def for_page_copies(of_step, act):
        b_of = lax.div(of_step, steps_per_batch)
        in_batch = lax.rem(of_step, steps_per_batch)
        ph_of, g_of = lax.div(in_batch, n_grp), lax.rem(in_batch, n_grp)
        to_slot = lax.rem(of_step, 2)
        for cache_ref, cache_pass in ((ck_ref, 0), (cv_ref, 1)):
            @pl.when(ph_of == cache_pass)
            def _(cache_ref=cache_ref):
                for i in range(gp):
                    page_id = pt_ref[b_of, g_of * gp + i]
                    act(pltpu.make_async_copy(cache_ref.at[layer, page_id], buf_ref.at[to_slot, i], sem.at[to_slot]))

    @pl.when(step == 0)
    def _():
        for_page_copies(step, lambda copy: copy.start())

    @pl.when(step + 1 < n_steps)
    def _():
        for_page_copies(step + 1, lambda copy: copy.start())

    for_page_copies(step, lambda copy: copy.wait())

    def page_rows():
        pages = [jnp.swapaxes(buf_ref[slot, i], 0, 1).astype(BF16) for i in range(gp)]
        return jnp.concatenate([jnp.concatenate([p[c] for c in range(n_kv)], axis=1) for p in pages], axis=0)

    def new_rows(ref):
        return jnp.concatenate([ref[...], jnp.zeros((LANES - dec_seq, n_kv * HEAD_DIM), F32)], axis=0).astype(BF16)

    @pl.when((ph == 0) & (g == 0))
    def _():
        q = q_ref[...]
        zero = jnp.zeros((dec_seq, HEAD_DIM), F32)
        for c in range(n_kv):
            q_rows = jnp.concatenate(
                [q[:, h * HEAD_DIM:(h + 1) * HEAD_DIM] if h // KV_GROUP == c else zero for h in range(n_heads)], axis=0)
            qt_ref[c * HEAD_DIM:(c + 1) * HEAD_DIM, :] = q_rows.T.astype(BF16)
        gate_ref[...] = jnp.zeros_like(gate_ref)
        bmax_ref[...] = jnp.zeros_like(bmax_ref)

    @pl.when(ph == 0)
    def _():
        s = _dot(page_rows(), qt_ref[...])
        s_ref[pl.ds(chunk_start, chunk), :] = s
        for n in range(blk_per_chunk):
            sb = s[n * MOBA_BLOCK:(n + 1) * MOBA_BLOCK]
            gate_ref[pl.ds(g * blk_per_chunk + n, 1), :] = jnp.mean(sb, axis=0, keepdims=True)
            bmax_ref[pl.ds(g * blk_per_chunk + n, 1), :] = jnp.max(sb, axis=0, keepdims=True)

    @pl.when((ph == 1) & (g == 0))
    def _():
        gate = gate_ref[...]
        n_iota = lax.broadcasted_iota(jnp.int32, gate.shape, 0)
        rank = jnp.zeros(gate.shape, jnp.int32)
        for j in range(n_past_blk):
            gj = gate[j:j + 1, :]
            rank = rank + jnp.where(gj > gate, 1, jnp.where((gj == gate) & (n_iota > j), 1, 0))
        sel = (rank < MOBA_TOPK) & (n_iota < n_past_blk)
        sel_ref[...] = jnp.where(sel, 1.0, 0.0)
        m = jnp.max(jnp.where(sel, bmax_ref[...], MASKED), axis=0, keepdims=True)
        so = _dot(new_rows(kn_ref), qt_ref[...])
        key_t = lax.broadcasted_iota(jnp.int32, so.shape, 0)
        q_t = lax.broadcasted_iota(jnp.int32, so.shape, 1) % dec_seq
        so = jnp.where(key_t <= q_t, so, MASKED)
        m = jnp.maximum(m, jnp.max(so, axis=0, keepdims=True))
        po = jnp.exp2((so - m) * EXP2_SCALE).T
        m_ref[...] = m
        l_ref[...] = jnp.sum(po, axis=-1, keepdims=True)
        acc_ref[...] = _dot(po.astype(BF16), new_rows(vn_ref))

    @pl.when(ph == 1)
    def _():
        p = jnp.exp2((s_ref[pl.ds(chunk_start, chunk), :] - m_ref[...]) * EXP2_SCALE)
        p = jnp.concatenate(
            [jnp.where(sel_ref[pl.ds(g * blk_per_chunk + n, 1), :] > 0.5, p[n * MOBA_BLOCK:(n + 1) * MOBA_BLOCK], 0.0)
             for n in range(blk_per_chunk)], axis=0).T
        l_ref[...] += jnp.sum(p, axis=-1, keepdims=True)
        acc_ref[...] += _dot(p.astype(BF16), page_rows())

    @pl.when((ph == 1) & (g == n_grp - 1))
    def _():
        out = acc_ref[...] / l_ref[...]
        for h in range(n_heads):
            c = h // KV_GROUP
            o_ref[:, h * HEAD_DIM:(h + 1) * HEAD_DIM] = out[h * dec_seq:(h + 1) * dec_seq,
                                                            c * HEAD_DIM:(c + 1) * HEAD_DIM]


def moba_sample(proj, cache_k, cache_v, page_table, layer, dec_batch, dec_seq, w_a, kv_w):
    page, n_kv = cache_k.shape[2], cache_k.shape[3]
    n_pages = page_table.shape[1]
    past = n_pages * page
    assert past % MOBA_BLOCK == 0 and dec_seq <= MOBA_BLOCK and dec_seq % SUBLANES == 0 and dec_seq <= LANES
    n_past_blk = past // MOBA_BLOCK
    n_heads = w_a // HEAD_DIM
    hq = n_heads * dec_seq
    gp = 32
    while n_pages % gp or (gp * page) % MOBA_BLOCK:
        gp //= 2
    n_grp = n_pages // gp
    blk_rows = -(-n_past_blk // SUBLANES) * SUBLANES
    k_col = w_a // kv_w
    grid_spec = pltpu.PrefetchScalarGridSpec(
        num_scalar_prefetch=1,
        grid=(dec_batch, 2, n_grp),
        in_specs=[
            pl.BlockSpec((dec_seq, w_a), lambda b, ph, g, pt: (b, 0)),
            pl.BlockSpec((dec_seq, kv_w), lambda b, ph, g, pt: (b, k_col)),
            pl.BlockSpec((dec_seq, kv_w), lambda b, ph, g, pt: (b, k_col + 1)),
            pl.BlockSpec(memory_space=pl.ANY),
            pl.BlockSpec(memory_space=pl.ANY),
        ],
        out_specs=pl.BlockSpec((dec_seq, w_a), lambda b, ph, g, pt: (b, 0)),
        scratch_shapes=[
            pltpu.VMEM((kv_w, hq), BF16),
            pltpu.VMEM((past, hq), F32),
            pltpu.VMEM((blk_rows, hq), F32),
            pltpu.VMEM((blk_rows, hq), F32),
            pltpu.VMEM((blk_rows, hq), F32),
            pltpu.VMEM((1, hq), F32),
            pltpu.VMEM((hq, 1), F32),
            pltpu.VMEM((hq, kv_w), F32),
            pltpu.VMEM((2, gp, page, n_kv, HEAD_DIM), F32),
            pltpu.SemaphoreType.DMA((2,)),
        ],
    )
    kernel = functools.partial(_moba_sample_kernel, layer=layer, gp=gp, n_grp=n_grp, n_batch=dec_batch,
                               n_past_blk=n_past_blk, dec_seq=dec_seq, n_heads=n_heads)
    return pl.pallas_call(
        kernel,
        grid_spec=grid_spec,
        out_shape=jax.ShapeDtypeStruct((dec_batch * dec_seq, w_a), F32),
        compiler_params=_cparams(("arbitrary", "arbitrary", "arbitrary"), 48),
        name="moba_sample",
    )(page_table, proj, proj, proj, cache_k, cache_v)


def _tril_weight(w):
    t_i = lax.broadcasted_iota(jnp.int32, w.shape, 0)
    s_i = lax.broadcasted_iota(jnp.int32, w.shape, 1)
    return jnp.where(t_i >= s_i, w, 0.0).astype(BF16)


def _sgu_kernel(bu_ref, bv_ref, w_ref, bcol_ref, lg_ref, lb_ref, o_ref):
    rows = bu_ref.shape[0]
    n_groups = w_ref.shape[0]
    for g in range(n_groups):
        sl = slice(g * LANES, (g + 1) * LANES)
        vn = _group_ln(jax.nn.gelu(bv_ref[:, sl]), lg_ref[:, sl], lb_ref[:, sl]).astype(BF16)
        u = jax.nn.gelu(bu_ref[:, sl])
        wg = _tril_weight(w_ref[g])
        for c in range(rows // SGU_CHUNK):
            rs = slice(c * SGU_CHUNK, (c + 1) * SGU_CHUNK)
            s = _dot(wg, vn[rs]) + bcol_ref[g]
            o_ref[rs, sl] = (u[rs] * s).astype(o_ref.dtype)


def sgu_prompt(proj, w, bcol, lg, lb, layer, rows, w_b, bu_off):
    m = proj.shape[0]
    n_groups = w_b // LANES
    bu_col = bu_off // w_b
    return pl.pallas_call(
        _sgu_kernel,
        grid=(m // rows,),
        in_specs=[
            pl.BlockSpec((rows, w_b), lambda i: (i, bu_col)),
            pl.BlockSpec((rows, w_b), lambda i: (i, bu_col + 1)),
            pl.BlockSpec((None, n_groups, SGU_CHUNK, SGU_CHUNK), lambda i: (layer, 0, 0, 0)),
            pl.BlockSpec((None, n_groups, SGU_CHUNK, LANES), lambda i: (layer, 0, 0, 0)),
            pl.BlockSpec((None, 1, w_b), lambda i: (layer, 0, 0)),
            pl.BlockSpec((None, 1, w_b), lambda i: (layer, 0, 0)),
        ],
        out_specs=pl.BlockSpec((rows, w_b), lambda i: (i, 0)),
        out_shape=jax.ShapeDtypeStruct((m, w_b), BF16),
        compiler_params=_cparams(("parallel",), 32),
        name="sgu_prompt",
    )(proj, proj, w, bcol, lg, lb)


CONV_PAD = 32
CONV_ROWS = 256


def _conv_tail(y, cb, lg, lb):
    return jax.nn.silu(_group_ln(y + cb, lg, lb))


def _conv_prompt_kernel(a_ref, gt_ref, w_ref, cb_ref, lg_ref, lb_ref, o_ref, tail_ref, pad_ref):
    seq = a_ref.shape[0]
    pad_ref[0:CONV_PAD, :] = jnp.zeros((CONV_PAD, LANES), F32)
    pad_ref[CONV_PAD:CONV_PAD + seq, :] = a_ref[...] * jax.nn.sigmoid(gt_ref[...])
    tail_ref[...] = pad_ref[CONV_PAD + seq - (CONV_W - 1):CONV_PAD + seq, :]
    w = w_ref[...]
    for c0 in range(0, seq, CONV_ROWS):
        acc = jnp.zeros((CONV_ROWS, LANES), F32)
        for j in range(CONV_W):
            start = CONV_PAD + c0 + j - (CONV_W - 1)
            acc = acc + pad_ref[start:start + CONV_ROWS, :] * w[j:j + 1, :]
        o_ref[c0:c0 + CONV_ROWS, :] = _conv_tail(acc, cb_ref[...], lg_ref[...], lb_ref[...]).astype(o_ref.dtype)


def conv_prompt(proj, w, cb, lg, lb, layer, batch, seq, w_c, cg_off):
    assert seq % CONV_ROWS == 0
    n_groups = w_c // LANES
    a_col = cg_off // LANES
    vec = lambda: pl.BlockSpec((None, 1, LANES), lambda b, c: (layer, 0, c))
    return pl.pallas_call(
        _conv_prompt_kernel,
        grid=(batch, n_groups),
        in_specs=[
            pl.BlockSpec((seq, LANES), lambda b, c: (b, a_col + c)),
            pl.BlockSpec((seq, LANES), lambda b, c: (b, a_col + n_groups + c)),
            pl.BlockSpec((None, CONV_W, LANES), lambda b, c: (layer, 0, c)),
            vec(), vec(), vec(),
        ],
        out_specs=[
            pl.BlockSpec((seq, LANES), lambda b, c: (b, c)),
            pl.BlockSpec((None, CONV_W - 1, LANES), lambda b, c: (b, 0, c)),
        ],
        out_shape=[
            jax.ShapeDtypeStruct((batch * seq, w_c), BF16),
            jax.ShapeDtypeStruct((batch, CONV_W - 1, w_c), F32),
        ],
        scratch_shapes=[pltpu.VMEM((CONV_PAD + seq, LANES), F32)],
        compiler_params=_cparams(("parallel", "parallel"), 32),
        name="conv_prompt",
    )(proj, proj, w, cb, lg, lb)


def _sample_mixer_kernel(bu_ref, bv_ref, a_ref, gt_ref, st_ref, sw_ref, bcol_ref, slg_ref, slb_ref,
                         cw_ref, cb_ref, clg_ref, clb_ref,
                         sgu_ref, conv_ref, cnew_ref, vn_ref, vpad_ref, cin_ref):
    dec_seq = bu_ref.shape[0]
    n_sgu = sw_ref.shape[0]
    n_conv = cw_ref.shape[1] // LANES
    state_rows = CONV_W - 1

    vpad_ref[...] = jnp.zeros_like(vpad_ref)
    for g in range(n_sgu):
        sl = slice(g * LANES, (g + 1) * LANES)
        vn = _group_ln(jax.nn.gelu(bv_ref[:, sl]), slg_ref[:, sl], slb_ref[:, sl])
        vn_ref[:, sl] = vn
        vpad_ref[0:dec_seq, sl] = vn
    for g in range(n_sgu):
        sl = slice(g * LANES, (g + 1) * LANES)
        s = _dot(_tril_weight(sw_ref[g]), vpad_ref[:, sl].astype(BF16)) + bcol_ref[g]
        sgu_ref[:, sl] = jax.nn.gelu(bu_ref[:, sl]) * s[0:dec_seq]

    cin_ref[0:state_rows, :] = st_ref[...]
    cin_ref[state_rows:state_rows + dec_seq, :] = a_ref[...] * jax.nn.sigmoid(gt_ref[...])
    cnew_ref[...] = cin_ref[dec_seq:dec_seq + state_rows, :]
    w = cw_ref[...]
    acc = jnp.zeros(a_ref.shape, F32)
    for j in range(CONV_W):
        acc = acc + cin_ref[j:j + dec_seq, :] * w[j:j + 1, :]
    for c in range(n_conv):
        sl = slice(c * LANES, (c + 1) * LANES)
        conv_ref[:, sl] = _conv_tail(acc[:, sl], cb_ref[:, sl], clg_ref[:, sl], clb_ref[:, sl])


def sample_mixer(proj, state_conv, sw, bcol, slg, slb, cw, cb, clg, clb, layer, dec_batch, dec_seq, w_b, w_c,
                 bu_off, cg_off):
    assert dec_seq <= SGU_CHUNK and dec_seq % SUBLANES == 0
    ms = dec_batch * dec_seq
    n_sgu = w_b // LANES
    bu_col = bu_off // w_b
    a_col = cg_off // w_c
    cin_rows = -(-(CONV_W - 1 + dec_seq) // SUBLANES) * SUBLANES
    lvec = lambda width: pl.BlockSpec((None, 1, width), lambda b: (layer, 0, 0))
    return pl.pallas_call(
        _sample_mixer_kernel,
        grid=(dec_batch,),
        in_specs=[
            pl.BlockSpec((dec_seq, w_b), lambda b: (b, bu_col)),
            pl.BlockSpec((dec_seq, w_b), lambda b: (b, bu_col + 1)),
            pl.BlockSpec((dec_seq, w_c), lambda b: (b, a_col)),
            pl.BlockSpec((dec_seq, w_c), lambda b: (b, a_col + 1)),
            pl.BlockSpec((None, None, CONV_W - 1, w_c), lambda b: (layer, b, 0, 0)),
            pl.BlockSpec((None, n_sgu, SGU_CHUNK, SGU_CHUNK), lambda b: (layer, 0, 0, 0)),
            pl.BlockSpec((None, n_sgu, SGU_CHUNK, LANES), lambda b: (layer, 0, 0, 0)),
            lvec(w_b), lvec(w_b),
            pl.BlockSpec((None, CONV_W, w_c), lambda b: (layer, 0, 0)),
            lvec(w_c), lvec(w_c), lvec(w_c),
        ],
        out_specs=[
            pl.BlockSpec((dec_seq, w_b), lambda b: (b, 0)),
            pl.BlockSpec((dec_seq, w_c), lambda b: (b, 0)),
            pl.BlockSpec((None, CONV_W - 1, w_c), lambda b: (b, 0, 0)),
            pl.BlockSpec((dec_seq, w_b), lambda b: (b, 0)),
        ],
        out_shape=[
            jax.ShapeDtypeStruct((ms, w_b), F32),
            jax.ShapeDtypeStruct((ms, w_c), F32),
            jax.ShapeDtypeStruct((dec_batch, CONV_W - 1, w_c), F32),
            jax.ShapeDtypeStruct((ms, w_b), F32),
        ],
        scratch_shapes=[pltpu.VMEM((SGU_CHUNK, w_b), F32), pltpu.VMEM((cin_rows, w_c), F32)],
        compiler_params=_cparams(("parallel",), 32),
        name="sample_mixer",
    )(proj, proj, proj, proj, state_conv, sw, bcol, slg, slb, cw, cb, clg, clb)


def _out_proj_kernel(a_ref, s_ref, c_ref, w_ref, x_ref, g_ref, o_ref, *w_copy):
    j = pl.program_id(1)
    tn = w_ref.shape[-1]
    w_a, w_b = a_ref.shape[1], s_ref.shape[1]
    w = _bf16_weight(w_ref, w_copy)
    acc = (_dot(a_ref[...].astype(BF16), w[0:w_a]) + _dot(s_ref[...].astype(BF16), w[w_a:w_a + w_b])
           + _dot(c_ref[...].astype(BF16), w[w_a + w_b:]))
    o_ref[:, pl.ds(pl.multiple_of(j * tn, tn), tn)] = acc

    @pl.when(j == pl.num_programs(1) - 1)
    def _():
        o_ref[...] = x_ref[...] + _rms(o_ref[...], g_ref[...])


def out_proj(attn, sgu, conv, w, x, g, layer, w_layer, tm, tn):
    m, d = x.shape
    w_a, w_b, w_c = attn.shape[1], sgu.shape[1], conv.shape[1]
    assert w_a + w_b + w_c == w.shape[1] and m % tm == 0 and d % tn == 0 and (w.dtype == BF16 or m == tm)
    copy_specs, copy_shapes = _weight_copy_out(w, (w.shape[1], tn), lambda i, j: (0, 0, j))
    out = pl.pallas_call(
        _out_proj_kernel,
        grid=(m // tm, d // tn),
        in_specs=[
            pl.BlockSpec((tm, w_a), lambda i, j: (i, 0)),
            pl.BlockSpec((tm, w_b), lambda i, j: (i, 0)),
            pl.BlockSpec((tm, w_c), lambda i, j: (i, 0)),
            pl.BlockSpec((None, w.shape[1], tn), lambda i, j: (w_layer, 0, j)),
            pl.BlockSpec((tm, d), lambda i, j: (i, 0)),
            pl.BlockSpec((None, 1, d), lambda i, j: (layer, 0, 0)),
        ],
        out_specs=[pl.BlockSpec((tm, d), lambda i, j: (i, 0))] + copy_specs,
        out_shape=[jax.ShapeDtypeStruct((m, d), F32)] + copy_shapes,
        compiler_params=_cparams(("parallel", "arbitrary"), 58),
        name=f"out_proj_m{m}",
    )(attn, sgu, conv, w, x, g)
    return out if copy_specs else out[0]


def _ffn_kernel(x_ref, gpre_ref, wup_ref, wdn_ref, gpost_ref, o_ref, *rest):
    *w_copies, xn_ref = rest
    f = pl.program_id(1)

    @pl.when(f == 0)
    def _():
        xn_ref[...] = _rms(x_ref[...], gpre_ref[...]).astype(BF16)
        o_ref[...] = jnp.zeros_like(o_ref)

    h = jnp.square(jnp.maximum(_dot(xn_ref[...], _bf16_weight(wup_ref, w_copies[:1])), 0.0)).astype(BF16)
    o_ref[...] += _dot(h, _bf16_weight(wdn_ref, w_copies[1:]))

    @pl.when(f == pl.num_programs(1) - 1)
    def _():
        o_ref[...] = x_ref[...] + _rms(o_ref[...], gpost_ref[...])


def ffn(x, gpre, wup, wdn, gpost, layer, w_layer, tm, tf):
    m, d = x.shape
    d_ff = wup.shape[-1]
    assert m % tm == 0 and d_ff % tf == 0 and wup.dtype == wdn.dtype and (wup.dtype == BF16 or m == tm)
    up_specs, up_shapes = _weight_copy_out(wup, (d, tf), lambda i, f: (0, 0, f))
    dn_specs, dn_shapes = _weight_copy_out(wdn, (tf, d), lambda i, f: (0, f, 0))
    out = pl.pallas_call(
        _ffn_kernel,
        grid=(m // tm, d_ff // tf),
        in_specs=[
            pl.BlockSpec((tm, d), lambda i, f: (i, 0), pipeline_mode=pl.Buffered(1)),
            pl.BlockSpec((None, 1, d), lambda i, f: (layer, 0, 0)),
            pl.BlockSpec((None, d, tf), lambda i, f: (w_layer, 0, f)),
            pl.BlockSpec((None, tf, d), lambda i, f: (w_layer, f, 0)),
            pl.BlockSpec((None, 1, d), lambda i, f: (layer, 0, 0)),
        ],
        out_specs=[pl.BlockSpec((tm, d), lambda i, f: (i, 0))] + up_specs + dn_specs,
        out_shape=[jax.ShapeDtypeStruct((m, d), F32)] + up_shapes + dn_shapes,
        scratch_shapes=[pltpu.VMEM((tm, d), BF16)],
        compiler_params=_cparams(("parallel", "arbitrary"), 58),
        name=f"ffn_m{m}",
    )(x, gpre, wup, wdn, gpost)
    return out if up_specs else out[0]


def _row_tile(m, target):
    t = min(m, target)
    while m % t:
        t //= 2
    return t


def kernel(x_prompt, x_sample, cache_k, cache_v, state_conv, page_table, g_mix_pre, w_in, sgu_norm_g, sgu_norm_b,
           sgu_w, sgu_b, conv_w, conv_b, conv_norm_g, conv_norm_b, w_out, g_mix_post, g_ffn_pre, w_up, w_down,
           g_ffn_post):
    batch, seq, d = x_prompt.shape
    dec_batch, dec_seq, _ = x_sample.shape
    depth = w_in.shape[0]
    w_a, w_b = d // 2, d // 4
    w_c = d - w_a - w_b
    n_kv = (w_a // HEAD_DIM) // KV_GROUP
    kv_w = n_kv * HEAD_DIM
    bu_off = w_a + 2 * kv_w
    cg_off = bu_off + 2 * w_b

    vec3 = lambda v: v.reshape(depth, 1, v.shape[-1])
    g_mix_pre3, g_mix_post3, g_ffn_pre3, g_ffn_post3 = map(vec3, (g_mix_pre, g_mix_post, g_ffn_pre, g_ffn_post))
    slg3, slb3, cb3, clg3, clb3 = map(vec3, (sgu_norm_g, sgu_norm_b, conv_b, conv_norm_g, conv_norm_b))
    bcol = jnp.broadcast_to(sgu_b[..., None], sgu_b.shape + (LANES,))

    mp, ms = batch * seq, dec_batch * dec_seq
    tm_p, tm_s = _row_tile(mp, 512), _row_tile(ms, 512)
    tn = _row_tile(d, 512)
    yp = x_prompt.reshape(mp, d)
    ys = x_sample.reshape(ms, d)
    cast_in_decode = ms == tm_s
    if not cast_in_decode:
        w_in, w_out, w_up, w_down = (w.astype(BF16) for w in (w_in, w_out, w_up, w_down))
    tf_s = _row_tile(w_up.shape[-1], 256) if cast_in_decode else tn

    def split(result, stacked, layer):
        return (result[0], result[1:], 0) if cast_in_decode else (result, stacked, layer)

    outs = [[] for _ in range(7)]
    for l in range(depth):
        sproj, (w_in_b,), wl = split(rms_inproj(ys, g_mix_pre3, w_in, l, l, tm_s, tn), (w_in,), l)
        sattn = moba_sample(sproj, cache_k, cache_v, page_table, l, dec_batch, dec_seq, w_a, kv_w)
        ssgu, sconv, sconv_new, svn = sample_mixer(sproj, state_conv, sgu_w, bcol, slg3, slb3, conv_w, cb3, clg3,
                                                   clb3, l, dec_batch, dec_seq, w_b, w_c, bu_off, cg_off)
        ys, (w_out_b,), _ = split(out_proj(sattn, ssgu, sconv, w_out, ys, g_mix_post3, l, l, tm_s, tn), (w_out,), l)
        ys, (w_up_b, w_dn_b), _ = split(ffn(ys, g_ffn_pre3, w_up, w_down, g_ffn_post3, l, l, tm_s, tf_s),
                                        (w_up, w_down), l)
        outs[3].append(sproj[:, w_a:w_a + kv_w].reshape(dec_batch, dec_seq, n_kv, HEAD_DIM))
        outs[4].append(sproj[:, w_a + kv_w:bu_off].reshape(dec_batch, dec_seq, n_kv, HEAD_DIM))
        outs[5].append(sconv_new)
        outs[6].append(svn.reshape(dec_batch, dec_seq, w_b))
        proj = rms_inproj(yp, g_mix_pre3, w_in_b, l, wl, tm_p, _row_tile(w_in.shape[-1], 1024))
        attn = moba_prompt(proj, batch, seq, w_a, kv_w)
        sgu = sgu_prompt(proj, sgu_w, bcol, slg3, slb3, l, _row_tile(mp, 512), w_b, bu_off)
        conv, conv_tail = conv_prompt(proj, conv_w, cb3, clg3, clb3, l, batch, seq, w_c, cg_off)
        yp = out_proj(attn, sgu, conv, w_out_b, yp, g_mix_post3, l, wl, tm_p, tn)
        yp = ffn(yp, g_ffn_pre3, w_up_b, w_dn_b, g_ffn_post3, l, wl, tm_p, tn)
        outs[0].append(proj[:, w_a:w_a + kv_w].reshape(batch, seq, n_kv, HEAD_DIM))
        outs[1].append(proj[:, w_a + kv_w:bu_off].reshape(batch, seq, n_kv, HEAD_DIM))
        outs[2].append(conv_tail)
    return (yp.reshape(batch, seq, d), ys.reshape(dec_batch, dec_seq, d)) + tuple(jnp.stack(o) for o in outs)
```

```python
import functools

import jax
import jax.numpy as jnp
from jax import lax
from jax.experimental import pallas as pl
from jax.experimental.pallas import tpu as pltpu

F32 = jnp.float32
BF16 = jnp.bfloat16

HEAD_DIM = 128
KV_GROUP = 4
MOBA_BLOCK = 256
MOBA_TOPK = 3
SGU_CHUNK = 128
CONV_W = 31
RMS_EPS = 1e-6
LN_EPS = 1e-5
MASKED = -1e30
EXP2_SCALE = HEAD_DIM ** -0.5 * 1.4426950408889634
LANES = 128
SUBLANES = 8
MIB = 1024 * 1024


def _cparams(sem, vmem_mib):
    return pltpu.CompilerParams(dimension_semantics=sem, vmem_limit_bytes=vmem_mib * MIB)


def _rms(x, g):
    ms = jnp.mean(x * x, axis=-1, keepdims=True)
    return x * lax.rsqrt(ms + RMS_EPS) * g


def _group_ln(x, g, b):
    mu = jnp.mean(x, axis=-1, keepdims=True)
    d = x - mu
    var = jnp.mean(d * d, axis=-1, keepdims=True)
    return d * lax.rsqrt(var + LN_EPS) * g + b


def _dot(a, b):
    return jnp.dot(a, b, preferred_element_type=F32)


def _dot_nt(a, b):
    return lax.dot_general(a, b, (((1,), (1,)), ((), ())), preferred_element_type=F32)


def _bf16_weight(w_ref, copy_refs):
    w = w_ref[...]
    if copy_refs:
        w = w.astype(BF16)
        copy_refs[0][...] = w
    return w


def _weight_copy_out(w, block, index_map):
    if w.dtype == BF16:
        return [], []
    return [pl.BlockSpec((None,) + block, index_map)], [jax.ShapeDtypeStruct((1,) + w.shape[1:], BF16)]


def _rms_inproj_kernel(x_ref, g_ref, w_ref, o_ref, *rest):
    *w_copy, h_ref = rest

    @pl.when(pl.program_id(1) == 0)
    def _():
        h_ref[...] = _rms(x_ref[...], g_ref[...]).astype(BF16)

    o_ref[...] = _dot(h_ref[...], _bf16_weight(w_ref, w_copy))


def rms_inproj(x, g, w, layer, w_layer, tm, tn):
    m, d = x.shape
    n = w.shape[-1]
    assert m % tm == 0 and n % tn == 0 and (w.dtype == BF16 or m == tm)
    copy_specs, copy_shapes = _weight_copy_out(w, (d, tn), lambda i, j: (0, 0, j))
    out = pl.pallas_call(
        _rms_inproj_kernel,
        grid=(m // tm, n // tn),
        in_specs=[
            pl.BlockSpec((tm, d), lambda i, j: (i, 0)),
            pl.BlockSpec((None, 1, d), lambda i, j: (layer, 0, 0)),
            pl.BlockSpec((None, d, tn), lambda i, j: (w_layer, 0, j)),
        ],
        out_specs=[pl.BlockSpec((tm, tn), lambda i, j: (i, j))] + copy_specs,
        out_shape=[jax.ShapeDtypeStruct((m, n), F32)] + copy_shapes,
        scratch_shapes=[pltpu.VMEM((tm, d), BF16)],
        compiler_params=_cparams(("parallel", "arbitrary"), 56),
        name=f"rms_inproj_m{m}",
    )(x, g, w)
    return out if copy_specs else out[0]


def _moba_prompt_kernel(q_ref, k_ref, v_ref, o_ref, kb_ref, vt_ref, s_ref, p_ref, *, nb):
    blk = MOBA_BLOCK
    rows = KV_GROUP * blk
    k = k_ref[...]
    kb_ref[...] = k.astype(BF16)
    vt_ref[...] = v_ref[...].T.astype(BF16)
    nb_pad = -(-nb // SUBLANES) * SUBLANES
    k_mean = jnp.concatenate(
        [jnp.mean(k[n * blk:(n + 1) * blk], axis=0, keepdims=True) for n in range(nb)]
        + [jnp.zeros((1, HEAD_DIM), F32)] * (nb_pad - nb), axis=0).astype(BF16)
    key_i = lax.broadcasted_iota(jnp.int32, (blk, rows), 0)
    row_i = lax.broadcasted_iota(jnp.int32, (blk, rows), 1) & (blk - 1)
    causal = key_i <= row_i

    for qi in range(nb):
        qs = q_ref[qi * blk:(qi + 1) * blk, :]
        qall = jnp.concatenate(
            [qs[:, g * HEAD_DIM:(g + 1) * HEAD_DIM] for g in range(KV_GROUP)], axis=0).astype(BF16)
        n_keys = (qi + 1) * blk
        s_ref[0:n_keys, :] = _dot_nt(kb_ref[0:n_keys, :], qall)

        def block_scores(j, qi=qi):
            s = s_ref[j * blk:(j + 1) * blk, :]
            return jnp.where(causal, s, MASKED) if j == qi else s

        bmax = [jnp.max(block_scores(j), axis=0, keepdims=True) for j in range(qi + 1)]
        if qi > MOBA_TOPK:
            gate = _dot_nt(k_mean, qall)
            n_iota = lax.broadcasted_iota(jnp.int32, gate.shape, 0)
            rank = jnp.zeros(gate.shape, jnp.int32)
            for j in range(qi):
                gj = gate[j:j + 1, :]
                rank = rank + jnp.where(gj > gate, 1, jnp.where((gj == gate) & (n_iota > j), 1, 0))
            picked = [rank[j:j + 1, :] < MOBA_TOPK for j in range(qi)]
            m = bmax[qi]
            for j in range(qi):
                m = jnp.maximum(m, jnp.where(picked[j], bmax[j], MASKED))
            off = [jnp.where(picked[j], m * EXP2_SCALE, -MASKED) for j in range(qi)] + [m * EXP2_SCALE]
        else:
            m = functools.reduce(jnp.maximum, bmax)
            off = [m * EXP2_SCALE] * (qi + 1)

        l = jnp.zeros((1, rows), F32)
        for j in range(qi + 1):
            p = jnp.exp2(block_scores(j) * EXP2_SCALE - off[j])
            l = l + jnp.sum(p, axis=0, keepdims=True)
            p_ref[j * blk:(j + 1) * blk, :] = p.astype(BF16)
        out = (_dot(vt_ref[:, 0:n_keys], p_ref[0:n_keys, :]) / l).T
        for g in range(KV_GROUP):
            o_ref[qi * blk:(qi + 1) * blk, g * HEAD_DIM:(g + 1) * HEAD_DIM] = (
                out[g * blk:(g + 1) * blk, :].astype(o_ref.dtype))


def moba_prompt(proj, batch, seq, w_a, kv_w):
    assert seq % MOBA_BLOCK == 0
    nq = seq // MOBA_BLOCK
    n_kv = kv_w // HEAD_DIM
    gw = KV_GROUP * HEAD_DIM
    rows = KV_GROUP * MOBA_BLOCK
    k_col = w_a // HEAD_DIM
    v_col = (w_a + kv_w) // HEAD_DIM
    return pl.pallas_call(
        functools.partial(_moba_prompt_kernel, nb=nq),
        grid=(batch, n_kv),
        in_specs=[
            pl.BlockSpec((seq, gw), lambda b, h: (b, h)),
            pl.BlockSpec((seq, HEAD_DIM), lambda b, h: (b, k_col + h)),
            pl.BlockSpec((seq, HEAD_DIM), lambda b, h: (b, v_col + h)),
        ],
        out_specs=pl.BlockSpec((seq, gw), lambda b, h: (b, h)),
        out_shape=jax.ShapeDtypeStruct((batch * seq, w_a), BF16),
        scratch_shapes=[
            pltpu.VMEM((seq, HEAD_DIM), BF16),
            pltpu.VMEM((HEAD_DIM, seq), BF16),
            pltpu.VMEM((seq, rows), F32),
            pltpu.VMEM((seq, rows), BF16),
        ],
        compiler_params=_cparams(("parallel", "parallel"), 56),
        name="moba_prompt",
    )(proj, proj, proj)


def _moba_sample_kernel(pt_ref, q_ref, kn_ref, vn_ref, ck_ref, cv_ref, o_ref,
                        qt_ref, s_ref, gate_ref, bmax_ref, sel_ref, m_ref, l_ref, acc_ref, buf_ref, sem,
                        *, layer, gp, n_grp, n_batch, n_past_blk, dec_seq, n_heads):
    ph = pl.program_id(1)
    g = pl.program_id(2)
    page, n_kv = buf_ref.shape[2], buf_ref.shape[3]
    chunk = gp * page
    blk_per_chunk = chunk // MOBA_BLOCK
    chunk_start = pl.multiple_of(g * chunk, chunk)
    steps_per_batch = 2 * n_grp
    n_steps = n_batch * steps_per_batch
    step = pl.program_id(0) * steps_per_batch + ph * n_grp + g
    slot = lax.rem(step, 2)

    def for_page_copies(of_step, act):
        b_of = lax.div(of_step, steps_per_batch)
        in_batch = lax.rem(of_step, steps_per_batch)
        ph_of, g_of = lax.div(in_batch, n_grp), lax.rem(in_batch, n_grp)
        to_slot = lax.rem(of_step, 2)
        for cache_ref, cache_pass in ((ck_ref, 0), (cv_ref, 1)):
            @pl.when(ph_of == cache_pass)
            def _(cache_ref=cache_ref):
                for i in range(gp):
                    page_id = pt_ref[b_of, g_of * gp + i]
                    act(pltpu.make_async_copy(cache_ref.at[layer, page_id], buf_ref.at[to_slot, i], sem.at[to_slot]))

    @pl.when(step == 0)
    def _():
        for_page_copies(step, lambda copy: copy.start())

    @pl.when(step + 1 < n_steps)
    def _():
        for_page_copies(step + 1, lambda copy: copy.start())

    for_page_copies(step, lambda copy: copy.wait())

    def page_rows():
        pages = [jnp.swapaxes(buf_ref[slot, i], 0, 1).astype(BF16) for i in range(gp)]
        return jnp.concatenate([jnp.concatenate([p[c] for c in range(n_kv)], axis=1) for p in pages], axis=0)

    def new_rows(ref):
        return jnp.concatenate([ref[...], jnp.zeros((LANES - dec_seq, n_kv * HEAD_DIM), F32)], axis=0).astype(BF16)

    @pl.when((ph == 0) & (g == 0))
    def _():
        q = q_ref[...]
        zero = jnp.zeros((dec_seq, HEAD_DIM), F32)
        for c in range(n_kv):
            q_rows = jnp.concatenate(
                [q[:, h * HEAD_DIM:(h + 1) * HEAD_DIM] if h // KV_GROUP == c else zero for h in range(n_heads)], axis=0)
            qt_ref[c * HEAD_DIM:(c + 1) * HEAD_DIM, :] = q_rows.T.astype(BF16)
        gate_ref[...] = jnp.zeros_like(gate_ref)
        bmax_ref[...] = jnp.zeros_like(bmax_ref)

    @pl.when(ph == 0)
    def _():
        s = _dot(page_rows(), qt_ref[...])
        s_ref[pl.ds(chunk_start, chunk), :] = s
        for n in range(blk_per_chunk):
            sb = s[n * MOBA_BLOCK:(n + 1) * MOBA_BLOCK]
            gate_ref[pl.ds(g * blk_per_chunk + n, 1), :] = jnp.mean(sb, axis=0, keepdims=True)
            bmax_ref[pl.ds(g * blk_per_chunk + n, 1), :] = jnp.max(sb, axis=0, keepdims=True)

    @pl.when((ph == 1) & (g == 0))
    def _():
        gate = gate_ref[...]
        n_iota = lax.broadcasted_iota(jnp.int32, gate.shape, 0)
        rank = jnp.zeros(gate.shape, jnp.int32)
        for j in range(n_past_blk):
            gj = gate[j:j + 1, :]
            rank = rank + jnp.where(gj > gate, 1, jnp.where((gj == gate) & (n_iota > j), 1, 0))
        sel = (rank < MOBA_TOPK) & (n_iota < n_past_blk)
        sel_ref[...] = jnp.where(sel, 1.0, 0.0)
        m = jnp.max(jnp.where(sel, bmax_ref[...], MASKED), axis=0, keepdims=True)
        so = _dot(new_rows(kn_ref), qt_ref[...])
        key_t = lax.broadcasted_iota(jnp.int32, so.shape, 0)
        q_t = lax.broadcasted_iota(jnp.int32, so.shape, 1) % dec_seq
        so = jnp.where(key_t <= q_t, so, MASKED)
        m = jnp.maximum(m, jnp.max(so, axis=0, keepdims=True))
        po = jnp.exp2((so - m) * EXP2_SCALE).T
        m_ref[...] = m
        l_ref[...] = jnp.sum(po, axis=-1, keepdims=True)
        acc_ref[...] = _dot(po.astype(BF16), new_rows(vn_ref))

    @pl.when(ph == 1)
    def _():
        p = jnp.exp2((s_ref[pl.ds(chunk_start, chunk), :] - m_ref[...]) * EXP2_SCALE)
        p = jnp.concatenate(
            [jnp.where(sel_ref[pl.ds(g * blk_per_chunk + n, 1), :] > 0.5, p[n * MOBA_BLOCK:(n + 1) * MOBA_BLOCK], 0.0)
             for n in range(blk_per_chunk)], axis=0).T
        l_ref[...] += jnp.sum(p, axis=-1, keepdims=True)
        acc_ref[...] += _dot(p.astype(BF16), page_rows())

    @pl.when((ph == 1) & (g == n_grp - 1))
    def _():
        out = acc_ref[...] / l_ref[...]
        for h in range(n_heads):
            c = h // KV_GROUP
            o_ref[:, h * HEAD_DIM:(h + 1) * HEAD_DIM] = out[h * dec_seq:(h + 1) * dec_seq,
                                                            c * HEAD_DIM:(c + 1) * HEAD_DIM]


def moba_sample(proj, cache_k, cache_v, page_table, layer, dec_batch, dec_seq, w_a, kv_w):
    page, n_kv = cache_k.shape[2], cache_k.shape[3]
    n_pages = page_table.shape[1]
    past = n_pages * page
    assert past % MOBA_BLOCK == 0 and dec_seq <= MOBA_BLOCK and dec_seq % SUBLANES == 0 and dec_seq <= LANES
    n_past_blk = past // MOBA_BLOCK
    n_heads = w_a // HEAD_DIM
    hq = n_heads * dec_seq
    gp = 64
    while n_pages % gp or (gp * page) % MOBA_BLOCK:
        gp //= 2
    n_grp = n_pages // gp
    blk_rows = -(-n_past_blk // SUBLANES) * SUBLANES
    k_col = w_a // kv_w
    grid_spec = pltpu.PrefetchScalarGridSpec(
        num_scalar_prefetch=1,
        grid=(dec_batch, 2, n_grp),
        in_specs=[
            pl.BlockSpec((dec_seq, w_a), lambda b, ph, g, pt: (b, 0)),
            pl.BlockSpec((dec_seq, kv_w), lambda b, ph, g, pt: (b, k_col)),
            pl.BlockSpec((dec_seq, kv_w), lambda b, ph, g, pt: (b, k_col + 1)),
            pl.BlockSpec(memory_space=pl.ANY),
            pl.BlockSpec(memory_space=pl.ANY),
        ],
        out_specs=pl.BlockSpec((dec_seq, w_a), lambda b, ph, g, pt: (b, 0)),
        scratch_shapes=[
            pltpu.VMEM((kv_w, hq), BF16),
            pltpu.VMEM((past, hq), F32),
            pltpu.VMEM((blk_rows, hq), F32),
            pltpu.VMEM((blk_rows, hq), F32),
            pltpu.VMEM((blk_rows, hq), F32),
            pltpu.VMEM((1, hq), F32),
            pltpu.VMEM((hq, 1), F32),
            pltpu.VMEM((hq, kv_w), F32),
            pltpu.VMEM((2, gp, page, n_kv, HEAD_DIM), F32),
            pltpu.SemaphoreType.DMA((2,)),
        ],
    )
    kernel = functools.partial(_moba_sample_kernel, layer=layer, gp=gp, n_grp=n_grp, n_batch=dec_batch,
                               n_past_blk=n_past_blk, dec_seq=dec_seq, n_heads=n_heads)
    return pl.pallas_call(
        kernel,
        grid_spec=grid_spec,
        out_shape=jax.ShapeDtypeStruct((dec_batch * dec_seq, w_a), F32),
        compiler_params=_cparams(("arbitrary", "arbitrary", "arbitrary"), 48),
        name="moba_sample",
    )(page_table, proj, proj, proj, cache_k, cache_v)


def _tril_weight(w):
    t_i = lax.broadcasted_iota(jnp.int32, w.shape, 0)
    s_i = lax.broadcasted_iota(jnp.int32, w.shape, 1)
    return jnp.where(t_i >= s_i, w, 0.0).astype(BF16)


def _sgu_kernel(bu_ref, bv_ref, w_ref, bcol_ref, lg_ref, lb_ref, o_ref):
    rows = bu_ref.shape[0]
    n_groups = w_ref.shape[0]
    for g in range(n_groups):
        sl = slice(g * LANES, (g + 1) * LANES)
        vn = _group_ln(jax.nn.gelu(bv_ref[:, sl]), lg_ref[:, sl], lb_ref[:, sl]).astype(BF16)
        u = jax.nn.gelu(bu_ref[:, sl])
        wg = _tril_weight(w_ref[g])
        for c in range(rows // SGU_CHUNK):
            rs = slice(c * SGU_CHUNK, (c + 1) * SGU_CHUNK)
            s = _dot(wg, vn[rs]) + bcol_ref[g]
            o_ref[rs, sl] = (u[rs] * s).astype(o_ref.dtype)


def sgu_prompt(proj, w, bcol, lg, lb, layer, rows, w_b, bu_off):
    m = proj.shape[0]
    n_groups = w_b // LANES
    bu_col = bu_off // w_b
    return pl.pallas_call(
        _sgu_kernel,
        grid=(m // rows,),
        in_specs=[
            pl.BlockSpec((rows, w_b), lambda i: (i, bu_col)),
            pl.BlockSpec((rows, w_b), lambda i: (i, bu_col + 1)),
            pl.BlockSpec((None, n_groups, SGU_CHUNK, SGU_CHUNK), lambda i: (layer, 0, 0, 0)),
            pl.BlockSpec((None, n_groups, SGU_CHUNK, LANES), lambda i: (layer, 0, 0, 0)),
            pl.BlockSpec((None, 1, w_b), lambda i: (layer, 0, 0)),
            pl.BlockSpec((None, 1, w_b), lambda i: (layer, 0, 0)),
        ],
        out_specs=pl.BlockSpec((rows, w_b), lambda i: (i, 0)),
        out_shape=jax.ShapeDtypeStruct((m, w_b), BF16),
        compiler_params=_cparams(("parallel",), 32),
        name="sgu_prompt",
    )(proj, proj, w, bcol, lg, lb)


CONV_PAD = 32
CONV_ROWS = 256


def _conv_tail(y, cb, lg, lb):
    return jax.nn.silu(_group_ln(y + cb, lg, lb))


def _conv_prompt_kernel(a_ref, gt_ref, w_ref, cb_ref, lg_ref, lb_ref, o_ref, tail_ref, pad_ref):
    seq = a_ref.shape[0]
    pad_ref[0:CONV_PAD, :] = jnp.zeros((CONV_PAD, LANES), F32)
    pad_ref[CONV_PAD:CONV_PAD + seq, :] = a_ref[...] * jax.nn.sigmoid(gt_ref[...])
    tail_ref[...] = pad_ref[CONV_PAD + seq - (CONV_W - 1):CONV_PAD + seq, :]
    w = w_ref[...]
    for c0 in range(0, seq, CONV_ROWS):
        acc = jnp.zeros((CONV_ROWS, LANES), F32)
        for j in range(CONV_W):
            start = CONV_PAD + c0 + j - (CONV_W - 1)
            acc = acc + pad_ref[start:start + CONV_ROWS, :] * w[j:j + 1, :]
        o_ref[c0:c0 + CONV_ROWS, :] = _conv_tail(acc, cb_ref[...], lg_ref[...], lb_ref[...]).astype(o_ref.dtype)


def conv_prompt(proj, w, cb, lg, lb, layer, batch, seq, w_c, cg_off):
    assert seq % CONV_ROWS == 0
    n_groups = w_c // LANES
    a_col = cg_off // LANES
    vec = lambda: pl.BlockSpec((None, 1, LANES), lambda b, c: (layer, 0, c))
    return pl.pallas_call(
        _conv_prompt_kernel,
        grid=(batch, n_groups),
        in_specs=[
            pl.BlockSpec((seq, LANES), lambda b, c: (b, a_col + c)),
            pl.BlockSpec((seq, LANES), lambda b, c: (b, a_col + n_groups + c)),
            pl.BlockSpec((None, CONV_W, LANES), lambda b, c: (layer, 0, c)),
            vec(), vec(), vec(),
        ],
        out_specs=[
            pl.BlockSpec((seq, LANES), lambda b, c: (b, c)),
            pl.BlockSpec((None, CONV_W - 1, LANES), lambda b, c: (b, 0, c)),
        ],
        out_shape=[
            jax.ShapeDtypeStruct((batch * seq, w_c), BF16),
            jax.ShapeDtypeStruct((batch, CONV_W - 1, w_c), F32),
        ],
        scratch_shapes=[pltpu.VMEM((CONV_PAD + seq, LANES), F32)],
        compiler_params=_cparams(("parallel", "parallel"), 32),
        name="conv_prompt",
    )(proj, proj, w, cb, lg, lb)


def _sample_mixer_kernel(bu_ref, bv_ref, a_ref, gt_ref, st_ref, sw_ref, bcol_ref, slg_ref, slb_ref,
                         cw_ref, cb_ref, clg_ref, clb_ref,
                         sgu_ref, conv_ref, cnew_ref, vn_ref, vpad_ref, cin_ref):
    dec_seq = bu_ref.shape[0]
    n_sgu = sw_ref.shape[0]
    n_conv = cw_ref.shape[1] // LANES
    state_rows = CONV_W - 1

    vpad_ref[...] = jnp.zeros_like(vpad_ref)
    for g in range(n_sgu):
        sl = slice(g * LANES, (g + 1) * LANES)
        vn = _group_ln(jax.nn.gelu(bv_ref[:, sl]), slg_ref[:, sl], slb_ref[:, sl])
        vn_ref[:, sl] = vn
        vpad_ref[0:dec_seq, sl] = vn
    for g in range(n_sgu):
        sl = slice(g * LANES, (g + 1) * LANES)
        s = _dot(_tril_weight(sw_ref[g]), vpad_ref[:, sl].astype(BF16)) + bcol_ref[g]
        sgu_ref[:, sl] = jax.nn.gelu(bu_ref[:, sl]) * s[0:dec_seq]

    cin_ref[0:state_rows, :] = st_ref[...]
    cin_ref[state_rows:state_rows + dec_seq, :] = a_ref[...] * jax.nn.sigmoid(gt_ref[...])
    cnew_ref[...] = cin_ref[dec_seq:dec_seq + state_rows, :]
    w = cw_ref[...]
    acc = jnp.zeros(a_ref.shape, F32)
    for j in range(CONV_W):
        acc = acc + cin_ref[j:j + dec_seq, :] * w[j:j + 1, :]
    for c in range(n_conv):
        sl = slice(c * LANES, (c + 1) * LANES)
        conv_ref[:, sl] = _conv_tail(acc[:, sl], cb_ref[:, sl], clg_ref[:, sl], clb_ref[:, sl])


def sample_mixer(proj, state_conv, sw, bcol, slg, slb, cw, cb, clg, clb, layer, dec_batch, dec_seq, w_b, w_c,
                 bu_off, cg_off):
    assert dec_seq <= SGU_CHUNK and dec_seq % SUBLANES == 0
    ms = dec_batch * dec_seq
    n_sgu = w_b // LANES
    bu_col = bu_off // w_b
    a_col = cg_off // w_c
    cin_rows = -(-(CONV_W - 1 + dec_seq) // SUBLANES) * SUBLANES
    lvec = lambda width: pl.BlockSpec((None, 1, width), lambda b: (layer, 0, 0))
    return pl.pallas_call(
        _sample_mixer_kernel,
        grid=(dec_batch,),
        in_specs=[
            pl.BlockSpec((dec_seq, w_b), lambda b: (b, bu_col)),
            pl.BlockSpec((dec_seq, w_b), lambda b: (b, bu_col + 1)),
            pl.BlockSpec((dec_seq, w_c), lambda b: (b, a_col)),
            pl.BlockSpec((dec_seq, w_c), lambda b: (b, a_col + 1)),
            pl.BlockSpec((None, None, CONV_W - 1, w_c), lambda b: (layer, b, 0, 0)),
            pl.BlockSpec((None, n_sgu, SGU_CHUNK, SGU_CHUNK), lambda b: (layer, 0, 0, 0)),
            pl.BlockSpec((None, n_sgu, SGU_CHUNK, LANES), lambda b: (layer, 0, 0, 0)),
            lvec(w_b), lvec(w_b),
            pl.BlockSpec((None, CONV_W, w_c), lambda b: (layer, 0, 0)),
            lvec(w_c), lvec(w_c), lvec(w_c),
        ],
        out_specs=[
            pl.BlockSpec((dec_seq, w_b), lambda b: (b, 0)),
            pl.BlockSpec((dec_seq, w_c), lambda b: (b, 0)),
            pl.BlockSpec((None, CONV_W - 1, w_c), lambda b: (b, 0, 0)),
            pl.BlockSpec((dec_seq, w_b), lambda b: (b, 0)),
        ],
        out_shape=[
            jax.ShapeDtypeStruct((ms, w_b), F32),
            jax.ShapeDtypeStruct((ms, w_c), F32),
            jax.ShapeDtypeStruct((dec_batch, CONV_W - 1, w_c), F32),
            jax.ShapeDtypeStruct((ms, w_b), F32),
        ],
        scratch_shapes=[pltpu.VMEM((SGU_CHUNK, w_b), F32), pltpu.VMEM((cin_rows, w_c), F32)],
        compiler_params=_cparams(("parallel",), 32),
        name="sample_mixer",
    )(proj, proj, proj, proj, state_conv, sw, bcol, slg, slb, cw, cb, clg, clb)


def _out_proj_kernel(a_ref, s_ref, c_ref, w_ref, x_ref, g_ref, o_ref, *w_copy):
    j = pl.program_id(1)
    tn = w_ref.shape[-1]
    w_a, w_b = a_ref.shape[1], s_ref.shape[1]
    w = _bf16_weight(w_ref, w_copy)
    acc = (_dot(a_ref[...].astype(BF16), w[0:w_a]) + _dot(s_ref[...].astype(BF16), w[w_a:w_a + w_b])
           + _dot(c_ref[...].astype(BF16), w[w_a + w_b:]))
    o_ref[:, pl.ds(pl.multiple_of(j * tn, tn), tn)] = acc

    @pl.when(j == pl.num_programs(1) - 1)
    def _():
        o_ref[...] = x_ref[...] + _rms(o_ref[...], g_ref[...])


def out_proj(attn, sgu, conv, w, x, g, layer, w_layer, tm, tn):
    m, d = x.shape
    w_a, w_b, w_c = attn.shape[1], sgu.shape[1], conv.shape[1]
    assert w_a + w_b + w_c == w.shape[1] and m % tm == 0 and d % tn == 0 and (w.dtype == BF16 or m == tm)
    copy_specs, copy_shapes = _weight_copy_out(w, (w.shape[1], tn), lambda i, j: (0, 0, j))
    out = pl.pallas_call(
        _out_proj_kernel,
        grid=(m // tm, d // tn),
        in_specs=[
            pl.BlockSpec((tm, w_a), lambda i, j: (i, 0)),
            pl.BlockSpec((tm, w_b), lambda i, j: (i, 0)),
            pl.BlockSpec((tm, w_c), lambda i, j: (i, 0)),
            pl.BlockSpec((None, w.shape[1], tn), lambda i, j: (w_layer, 0, j)),
            pl.BlockSpec((tm, d), lambda i, j: (i, 0)),
            pl.BlockSpec((None, 1, d), lambda i, j: (layer, 0, 0)),
        ],
        out_specs=[pl.BlockSpec((tm, d), lambda i, j: (i, 0))] + copy_specs,
        out_shape=[jax.ShapeDtypeStruct((m, d), F32)] + copy_shapes,
        compiler_params=_cparams(("parallel", "arbitrary"), 58),
        name=f"out_proj_m{m}",
    )(attn, sgu, conv, w, x, g)
    return out if copy_specs else out[0]


def _ffn_kernel(x_ref, gpre_ref, wup_ref, wdn_ref, gpost_ref, o_ref, *rest):
    *w_copies, xn_ref = rest
    f = pl.program_id(1)

    @pl.when(f == 0)
    def _():
        xn_ref[...] = _rms(x_ref[...], gpre_ref[...]).astype(BF16)
        o_ref[...] = jnp.zeros_like(o_ref)

    h = jnp.square(jnp.maximum(_dot(xn_ref[...], _bf16_weight(wup_ref, w_copies[:1])), 0.0)).astype(BF16)
    o_ref[...] += _dot(h, _bf16_weight(wdn_ref, w_copies[1:]))

    @pl.when(f == pl.num_programs(1) - 1)
    def _():
        o_ref[...] = x_ref[...] + _rms(o_ref[...], gpost_ref[...])


def ffn(x, gpre, wup, wdn, gpost, layer, w_layer, tm, tf):
    m, d = x.shape
    d_ff = wup.shape[-1]
    assert m % tm == 0 and d_ff % tf == 0 and wup.dtype == wdn.dtype and (wup.dtype == BF16 or m == tm)
    up_specs, up_shapes = _weight_copy_out(wup, (d, tf), lambda i, f: (0, 0, f))
    dn_specs, dn_shapes = _weight_copy_out(wdn, (tf, d), lambda i, f: (0, f, 0))
    out = pl.pallas_call(
        _ffn_kernel,
        grid=(m // tm, d_ff // tf),
        in_specs=[
            pl.BlockSpec((tm, d), lambda i, f: (i, 0), pipeline_mode=pl.Buffered(1)),
            pl.BlockSpec((None, 1, d), lambda i, f: (layer, 0, 0)),
            pl.BlockSpec((None, d, tf), lambda i, f: (w_layer, 0, f)),
            pl.BlockSpec((None, tf, d), lambda i, f: (w_layer, f, 0)),
            pl.BlockSpec((None, 1, d), lambda i, f: (layer, 0, 0)),
        ],
        out_specs=[pl.BlockSpec((tm, d), lambda i, f: (i, 0))] + up_specs + dn_specs,
        out_shape=[jax.ShapeDtypeStruct((m, d), F32)] + up_shapes + dn_shapes,
        scratch_shapes=[pltpu.VMEM((tm, d), BF16)],
        compiler_params=_cparams(("parallel", "arbitrary"), 58),
        name=f"ffn_m{m}",
    )(x, gpre, wup, wdn, gpost)
    return out if up_specs else out[0]


def _row_tile(m, target):
    t = min(m, target)
    while m % t:
        t //= 2
    return t


def kernel(x_prompt, x_sample, cache_k, cache_v, state_conv, page_table, g_mix_pre, w_in, sgu_norm_g, sgu_norm_b,
           sgu_w, sgu_b, conv_w, conv_b, conv_norm_g, conv_norm_b, w_out, g_mix_post, g_ffn_pre, w_up, w_down,
           g_ffn_post):
    batch, seq, d = x_prompt.shape
    dec_batch, dec_seq, _ = x_sample.shape
    depth = w_in.shape[0]
    w_a, w_b = d // 2, d // 4
    w_c = d - w_a - w_b
    n_kv = (w_a // HEAD_DIM) // KV_GROUP
    kv_w = n_kv * HEAD_DIM
    bu_off = w_a + 2 * kv_w
    cg_off = bu_off + 2 * w_b

    vec3 = lambda v: v.reshape(depth, 1, v.shape[-1])
    g_mix_pre3, g_mix_post3, g_ffn_pre3, g_ffn_post3 = map(vec3, (g_mix_pre, g_mix_post, g_ffn_pre, g_ffn_post))
    slg3, slb3, cb3, clg3, clb3 = map(vec3, (sgu_norm_g, sgu_norm_b, conv_b, conv_norm_g, conv_norm_b))
    bcol = jnp.broadcast_to(sgu_b[..., None], sgu_b.shape + (LANES,))

    mp, ms = batch * seq, dec_batch * dec_seq
    tm_p, tm_s = _row_tile(mp, 512), _row_tile(ms, 512)
    tn = _row_tile(d, 512)
    yp = x_prompt.reshape(mp, d)
    ys = x_sample.reshape(ms, d)
    cast_in_decode = ms == tm_s
    if not cast_in_decode:
        w_in, w_out, w_up, w_down = (w.astype(BF16) for w in (w_in, w_out, w_up, w_down))
    tf_s = _row_tile(w_up.shape[-1], 256) if cast_in_decode else tn

    def split(result, stacked, layer):
        return (result[0], result[1:], 0) if cast_in_decode else (result, stacked, layer)

    outs = [[] for _ in range(7)]
    for l in range(depth):
        sproj, (w_in_b,), wl = split(rms_inproj(ys, g_mix_pre3, w_in, l, l, tm_s, tn), (w_in,), l)
        sattn = moba_sample(sproj, cache_k, cache_v, page_table, l, dec_batch, dec_seq, w_a, kv_w)
        ssgu, sconv, sconv_new, svn = sample_mixer(sproj, state_conv, sgu_w, bcol, slg3, slb3, conv_w, cb3, clg3,
                                                   clb3, l, dec_batch, dec_seq, w_b, w_c, bu_off, cg_off)
        ys, (w_out_b,), _ = split(out_proj(sattn, ssgu, sconv, w_out, ys, g_mix_post3, l, l, tm_s, tn), (w_out,), l)
        ys, (w_up_b, w_dn_b), _ = split(ffn(ys, g_ffn_pre3, w_up, w_down, g_ffn_post3, l, l, tm_s, tf_s),
                                        (w_up, w_down), l)
        outs[3].append(sproj[:, w_a:w_a + kv_w].reshape(dec_batch, dec_seq, n_kv, HEAD_DIM))
        outs[4].append(sproj[:, w_a + kv_w:bu_off].reshape(dec_batch, dec_seq, n_kv, HEAD_DIM))
        outs[5].append(sconv_new)
        outs[6].append(svn.reshape(dec_batch, dec_seq, w_b))
        proj = rms_inproj(yp, g_mix_pre3, w_in_b, l, wl, tm_p, _row_tile(w_in.shape[-1], 1024))
        attn = moba_prompt(proj, batch, seq, w_a, kv_w)
        sgu = sgu_prompt(proj, sgu_w, bcol, slg3, slb3, l, _row_tile(mp, 512), w_b, bu_off)
        conv, conv_tail = conv_prompt(proj, conv_w, cb3, clg3, clb3, l, batch, seq, w_c, cg_off)
        yp = out_proj(attn, sgu, conv, w_out_b, yp, g_mix_post3, l, wl, tm_p, tn)
        yp = ffn(yp, g_ffn_pre3, w_up_b, w_dn_b, g_ffn_post3, l, wl, tm_p, tn)
        outs[0].append(proj[:, w_a:w_a + kv_w].reshape(batch, seq, n_kv, HEAD_DIM))
        outs[1].append(proj[:, w_a + kv_w:bu_off].reshape(batch, seq, n_kv, HEAD_DIM))
        outs[2].append(conv_tail)
    return (yp.reshape(batch, seq, d), ys.reshape(dec_batch, dec_seq, d)) + tuple(jnp.stack(o) for o in outs)
```
